```python
import jax
import jax.numpy as jnp
from jax import lax
import numpy as np

D_MODEL = 1024
BATCH = 4
SEQ = 8192
DEPTH = 1
DEC_BATCH = 128
DEC_SEQ = 8
PAST_LEN = 16384
PAGE_SIZE = 128

H_A = 4
DK_A = 128
DV_A = 256
GLA_LR = 16
GLA_TAU = 16.0
GLA_CHUNK = 64
H_B = 8
Q_LORA = 384
KV_LORA = 256
QK_NOPE = 128
QK_ROPE = 64
V_HEAD = 128
ROPE_THETA = 10000.0
Q_BLOCK = 128
ATTN_SCALE = (QK_NOPE + QK_ROPE) ** -0.5
N_EXPERTS = 256
TOP_K = 8
N_GROUPS = 8
TOPK_GROUPS = 4
D_EXPERT = 256
D_SHARED = 256
ROUTED_SCALE = 2.5
MOE_BLOCK = 64
ALPHA = (2.0 * DEPTH) ** 0.25
BETA = (8.0 * DEPTH) ** -0.25
EPS = 1e-6
IN_SIZES = (Q_LORA, KV_LORA, QK_ROPE, H_A * DK_A, H_A * DK_A, H_A * DV_A, H_A * DV_A, GLA_LR, D_MODEL, D_MODEL)

kernel_name = 'hybrid_gla_mla_moe_decoder_step'


def layer_norm(x, g, b):
    xf = x.astype(jnp.float32)
    mu = jnp.mean(xf, axis=-1, keepdims=True)
    var = jnp.mean(jnp.square(xf - mu), axis=-1, keepdims=True)
    return ((xf - mu) * lax.rsqrt(var + EPS) * g + b).astype(x.dtype)


def rms_norm(x, g):
    xf = x.astype(jnp.float32)
    return (xf * lax.rsqrt(jnp.mean(jnp.square(xf), axis=-1, keepdims=True) + EPS) * g).astype(x.dtype)


def rope(x, pos):
    half = x.shape[-1] // 2
    freqs = ROPE_THETA ** (-jnp.arange(half, dtype=jnp.float32) / half)
    ang = pos.astype(jnp.float32)[:, None] * freqs
    cos, sin = jnp.cos(ang)[:, None, :], jnp.sin(ang)[:, None, :]
    x1 = x[..., :half].astype(jnp.float32)
    x2 = x[..., half:].astype(jnp.float32)
    return jnp.concatenate([x1 * cos - x2 * sin, x1 * sin + x2 * cos], axis=-1).astype(x.dtype)


def adaln(c, w_ada, b_ada):
    mod = jnp.einsum('bd,de->be', jax.nn.silu(c), w_ada) + b_ada
    return jnp.split(mod[:, None, :], 6, axis=-1)


def mixer_projections(h, pos, w_in, g_q_norm, w_uq, g_kv_norm, w_uk, w_gla_f2, b_gla_f):
    B, S, _ = h.shape
    z = jnp.einsum('bsd,de->bse', h, w_in)
    cuts, acc = [], 0
    for size in IN_SIZES[:-1]:
        acc += size
        cuts.append(acc)
    q_a, kv_a, kr_raw, qa, ka, va, ga, fa, gate_a, gate_b = jnp.split(z, cuts, axis=-1)
    q = jnp.einsum('bsr,re->bse', rms_norm(q_a, g_q_norm), w_uq).reshape(B, S, H_B, QK_NOPE + QK_ROPE)
    q_rope = rope(q[..., QK_NOPE:], pos)
    q_lat = jnp.einsum('bshn,chn->bshc', q[..., :QK_NOPE], w_uk)
    ckv = rms_norm(kv_a, g_kv_norm)
    k_rope = rope(kr_raw[:, :, None, :], pos)[:, :, 0]
    q_g = qa.reshape(B, S, H_A, DK_A) * DK_A ** -0.5
    k_g = ka.reshape(B, S, H_A, DK_A)
    v_g = va.reshape(B, S, H_A, DV_A)
    f_pre = (jnp.einsum('bsr,re->bse', fa, w_gla_f2) + b_gla_f).astype(jnp.float32)
    logf = (jax.nn.log_sigmoid(f_pre) / GLA_TAU).reshape(B, S, H_A, DK_A)
    return q_lat, q_rope, ckv, k_rope, q_g, k_g, v_g, logf, ga, gate_a, gate_b


def gla_recurrence(q, k, v, logf, s0, chunk):
    B, T, H, _ = q.shape
    DV = v.shape[-1]
    pad = (-T) % chunk
    def prep(a):
        return jnp.pad(a.astype(jnp.float32), ((0, 0), (0, pad), (0, 0), (0, 0)))
    q, k, v, logf = prep(q), prep(k), prep(v), prep(logf)
    n = (T + pad) // chunk
    def to_chunks(a):
        return a.reshape(B, n, chunk, H, a.shape[-1]).swapaxes(0, 1)
    causal = jnp.tril(jnp.ones((chunk, chunk), bool))[None, :, :, None, None]
    def step(S, xs):
        qc, kc, vc, gc = xs
        b = jnp.cumsum(gc, axis=1)
        inter = jnp.einsum('bthk,bhkv->bthv', qc * jnp.exp(b), S)
        diff = b[:, :, None] - b[:, None, :]
        decay = jnp.where(causal, jnp.exp(jnp.where(causal, diff, 0.0)), 0.0)
        A = jnp.einsum('bthk,bshk,btshk->bhts', qc, kc, decay)
        intra = jnp.einsum('bhts,bshv->bthv', A, vc)
        b_last = b[:, -1]
        S_new = jnp.exp(b_last)[..., None] * S + jnp.einsum('bshk,bshv->bhkv', kc * jnp.exp(b_last[:, None] - b), vc)
        return S_new, inter + intra
    S, o = lax.scan(step, s0.astype(jnp.float32), (to_chunks(q), to_chunks(k), to_chunks(v), to_chunks(logf)))
    o = o.swapaxes(0, 1).reshape(B, n * chunk, H, DV)[:, :T]
    return o, S


def mla_prompt_attend(q_lat, q_rope, ckv, k_rope):
    B, S, H, C = q_lat.shape
    nblk = S // Q_BLOCK
    def blocks(a):
        return a.reshape(B, nblk, Q_BLOCK, *a.shape[2:]).swapaxes(0, 1)
    key_pos = jnp.arange(S)
    def one_block(xs):
        ql, qr, blk = xs
        s = (jnp.einsum('bqhc,bkc->bhqk', ql, ckv, preferred_element_type=jnp.float32)
             + jnp.einsum('bqhr,bkr->bhqk', qr, k_rope, preferred_element_type=jnp.float32)) * ATTN_SCALE
        q_pos = blk * Q_BLOCK + jnp.arange(Q_BLOCK)
        s = jnp.where(key_pos[None, :] <= q_pos[:, None], s, -jnp.inf)
        p = jax.nn.softmax(s, axis=-1)
        return jnp.einsum('bhqk,bkc->bqhc', p, ckv, preferred_element_type=jnp.float32)
    o = lax.map(one_block, (blocks(q_lat), blocks(q_rope), jnp.arange(nblk)))
    return o.swapaxes(0, 1).reshape(B, S, H, C)


def mla_sample_attend(q_lat, q_rope, ckv, k_rope, cache_kv_latent, cache_k_rope, page_table, layer):
    T = q_lat.shape[1]
    def scores(ck, kr):
        return (jnp.einsum('bqhc,bkc->bhqk', q_lat, ck, preferred_element_type=jnp.float32)
                + jnp.einsum('bqhr,bkr->bhqk', q_rope, kr, preferred_element_type=jnp.float32)) * ATTN_SCALE
    causal = jnp.tril(jnp.ones((T, T), bool))
    s = jnp.where(causal, scores(ckv, k_rope), -jnp.inf)
    m = jnp.max(s, axis=-1)
    p = jnp.exp(s - m[..., None])
    l = jnp.sum(p, axis=-1)
    acc = jnp.einsum('bhqk,bkc->bhqc', p, ckv, preferred_element_type=jnp.float32)
    def page_step(carry, phys):
        m, l, acc = carry
        ck = cache_kv_latent[layer, phys]
        kr = cache_k_rope[layer, phys]
        s = scores(ck, kr)
        m_new = jnp.maximum(m, jnp.max(s, axis=-1))
        corr = jnp.exp(m - m_new)
        p = jnp.exp(s - m_new[..., None])
        l = l * corr + jnp.sum(p, axis=-1)
        acc = acc * corr[..., None] + jnp.einsum('bhqk,bkc->bhqc', p, ck, preferred_element_type=jnp.float32)
        return (m_new, l, acc), None
    (m, l, acc), _ = lax.scan(page_step, (m, l, acc), page_table.T)
    return (acc / l[..., None]).swapaxes(1, 2)


def mixer_merge(o_lat, o_gla, ga, gate_a, gate_b, w_uv, g_gla_norm, w_a_out, w_b_out, w_o):
    B, S = o_lat.shape[:2]
    v_b = jnp.einsum('bshc,chv->bshv', o_lat, w_uv).reshape(B, S, H_B * V_HEAD)
    y_b = jnp.einsum('bse,ed->bsd', v_b, w_b_out)
    g_a = rms_norm(o_gla, g_gla_norm) * jax.nn.silu(ga.reshape(B, S, H_A, DV_A))
    y_a = jnp.einsum('bse,ed->bsd', g_a.reshape(B, S, H_A * DV_A), w_a_out)
    merged = jax.nn.sigmoid(gate_a) * y_a + jax.nn.sigmoid(gate_b) * y_b
    return jnp.einsum('bsd,de->bse', merged, w_o)


def routed_experts(hf, w_router, router_bias, w_e_gate, w_e_up, w_e_down):
    T, D = hf.shape
    scores = jax.nn.sigmoid(jnp.einsum('td,de->te', hf, w_router, preferred_element_type=jnp.float32))
    sel = scores + router_bias.astype(jnp.float32)
    grp = sel.reshape(T, N_GROUPS, N_EXPERTS // N_GROUPS)
    grp_score = jnp.sum(lax.top_k(grp, 2)[0], axis=-1)
    _, top_g = lax.top_k(grp_score, TOPK_GROUPS)
    gmask = jnp.sum(jax.nn.one_hot(top_g, N_GROUPS, dtype=jnp.float32), axis=1) > 0
    sel = jnp.where(jnp.repeat(gmask, N_EXPERTS // N_GROUPS, axis=1), sel, -jnp.inf)
    _, top_e = lax.top_k(sel, TOP_K)
    wts = jnp.take_along_axis(scores, top_e, axis=1)
    wts = wts / jnp.sum(wts, axis=-1, keepdims=True) * ROUTED_SCALE
    M = T * TOP_K
    flat_e = top_e.reshape(-1)
    flat_t = jnp.repeat(jnp.arange(T, dtype=jnp.int32), TOP_K)
    flat_w = wts.reshape(-1)
    order = jnp.argsort(flat_e)
    se, st, sw = flat_e[order], flat_t[order], flat_w[order]
    counts = jnp.bincount(flat_e, length=N_EXPERTS)
    padded = (counts + MOE_BLOCK - 1) // MOE_BLOCK * MOE_BLOCK
    pend = jnp.cumsum(padded)
    pstart = pend - padded
    ustart = jnp.cumsum(counts) - counts
    pos = pstart[se] + jnp.arange(M) - ustart[se]
    m_pad = -(-(M + N_EXPERTS * (MOE_BLOCK - 1)) // MOE_BLOCK) * MOE_BLOCK
    n_blk = m_pad // MOE_BLOCK
    tok_pad = jnp.full((m_pad,), T, jnp.int32).at[pos].set(st)
    w_pad = jnp.zeros((m_pad,), jnp.float32).at[pos].set(sw)
    blk_e = jnp.minimum(jnp.searchsorted(pend, jnp.arange(n_blk) * MOE_BLOCK, side='right'), N_EXPERTS - 1)
    h_ext = jnp.concatenate([hf, jnp.zeros((1, D), hf.dtype)], axis=0)
    def body(acc, xs):
        tok, wt, e = xs
        xb = h_ext[tok]
        yb = (jax.nn.silu(xb @ w_e_gate[e]) * (xb @ w_e_up[e])) @ w_e_down[e]
        return acc.at[tok].add(yb.astype(jnp.float32) * wt[:, None]), None
    acc, _ = lax.scan(body, jnp.zeros((T + 1, D), jnp.float32),
                      (tok_pad.reshape(n_blk, MOE_BLOCK), w_pad.reshape(n_blk, MOE_BLOCK), blk_e))
    return acc[:T]


def moe_ffn(h, w_router, router_bias, w_e_gate, w_e_up, w_e_down, w_s_gate, w_s_up, w_s_down):
    B, S, D = h.shape
    hf = h.reshape(B * S, D)
    shared = (jax.nn.silu(hf @ w_s_gate) * (hf @ w_s_up)) @ w_s_down
    routed = routed_experts(hf, w_router, router_bias, w_e_gate, w_e_up, w_e_down)
    return (shared + routed.astype(h.dtype)).reshape(B, S, D)


def trunk_layer(x, c, pos, gla_s0, gla_chunk, mla_attend, w):
    sh_m, sc_m, g_m, sh_f, sc_f, g_f = adaln(c, w['w_ada'], w['b_ada'])
    h = x * (1.0 + sc_m) + sh_m
    q_lat, q_rope, ckv, k_rope, q_g, k_g, v_g, logf, ga, gate_a, gate_b = mixer_projections(
        h, pos, w['w_in'], w['g_q_norm'], w['w_uq'], w['g_kv_norm'], w['w_uk'], w['w_gla_f2'], w['b_gla_f'])
    o_lat = mla_attend(q_lat, q_rope, ckv, k_rope)
    o_gla, s_final = gla_recurrence(q_g, k_g, v_g, logf, gla_s0, gla_chunk)
    mix = mixer_merge(o_lat, o_gla, ga, gate_a, gate_b, w['w_uv'], w['g_gla_norm'], w['w_a_out'], w['w_b_out'], w['w_o'])
    x = layer_norm(ALPHA * x + g_m * mix.astype(x.dtype), w['ln1_g'], w['ln1_b'])
    h = x * (1.0 + sc_f) + sh_f
    ffn = moe_ffn(h, w['w_router'], w['router_bias'], w['w_e_gate'], w['w_e_up'], w['w_e_down'],
                  w['w_s_gate'], w['w_s_up'], w['w_s_down'])
    x = layer_norm(ALPHA * x + g_f * ffn, w['ln2_g'], w['ln2_b'])
    return x, ckv, k_rope, s_final


def setup_inputs(seed: int = 0) -> dict:
    key = jax.random.key(seed)
    ks = iter(jax.random.split(key, 64))
    f32 = jnp.float32
    def nrm(shape, scale):
        return jax.random.normal(next(ks), shape, f32) * scale
    L = DEPTH
    n_pages = PAST_LEN // PAGE_SIZE
    n_phys = (DEC_BATCH * n_pages * 5) // 4
    x_prompt = nrm((BATCH, SEQ, D_MODEL), 1.0)
    x_sample = nrm((DEC_BATCH, DEC_SEQ, D_MODEL), 1.0)
    c_prompt = nrm((BATCH, D_MODEL), 1.0)
    c_sample = nrm((DEC_BATCH, D_MODEL), 1.0)
    cache_kv_latent = nrm((L, n_phys, PAGE_SIZE, KV_LORA), 1.0)
    cache_k_rope = nrm((L, n_phys, PAGE_SIZE, QK_ROPE), 1.0)
    state_gla = nrm((L, DEC_BATCH, H_A, DK_A, DV_A), 0.1)
    page_table = jax.random.permutation(next(ks), n_phys)[:DEC_BATCH * n_pages].reshape(DEC_BATCH, n_pages).astype(jnp.int32)
    dsc = D_MODEL ** -0.5
    w_ada = nrm((L, D_MODEL, 6 * D_MODEL), 0.5 * dsc)
    b_ada = nrm((L, 6 * D_MODEL), 0.02)
    v_index = 5
    w_in = jnp.concatenate([nrm((L, D_MODEL, size), dsc * (BETA if i == v_index else 1.0))
                            for i, size in enumerate(IN_SIZES)], axis=-1)
    g_q_norm = 1.0 + nrm((L, Q_LORA), 0.02)
    w_uq = nrm((L, Q_LORA, H_B * (QK_NOPE + QK_ROPE)), Q_LORA ** -0.5)
    g_kv_norm = 1.0 + nrm((L, KV_LORA), 0.02)
    w_uk = nrm((L, KV_LORA, H_B, QK_NOPE), KV_LORA ** -0.5)
    w_uv = nrm((L, KV_LORA, H_B, V_HEAD), KV_LORA ** -0.5 * BETA)
    w_gla_f2 = nrm((L, GLA_LR, H_A * DK_A), GLA_LR ** -0.5)
    b_gla_f = nrm((L, H_A * DK_A), 0.1)
    g_gla_norm = 1.0 + nrm((L, DV_A), 0.02)
    w_a_out = nrm((L, H_A * DV_A, D_MODEL), (H_A * DV_A) ** -0.5 * BETA)
    w_b_out = nrm((L, H_B * V_HEAD, D_MODEL), (H_B * V_HEAD) ** -0.5 * BETA)
    w_o = nrm((L, D_MODEL, D_MODEL), dsc * BETA)
    ln1_g = 1.0 + nrm((L, D_MODEL), 0.02)
    ln1_b = nrm((L, D_MODEL), 0.02)
    w_router = nrm((L, D_MODEL, N_EXPERTS), dsc)
    router_bias = nrm((L, N_EXPERTS), 0.01)
    w_e_gate = nrm((L, N_EXPERTS, D_MODEL, D_EXPERT), dsc)
    w_e_up = nrm((L, N_EXPERTS, D_MODEL, D_EXPERT), dsc)
    w_e_down = nrm((L, N_EXPERTS, D_EXPERT, D_MODEL), D_EXPERT ** -0.5 * BETA)
    w_s_gate = nrm((L, D_MODEL, D_SHARED), dsc)
    w_s_up = nrm((L, D_MODEL, D_SHARED), dsc)
    w_s_down = nrm((L, D_SHARED, D_MODEL), D_SHARED ** -0.5 * BETA)
    ln2_g = 1.0 + nrm((L, D_MODEL), 0.02)
    ln2_b = nrm((L, D_MODEL), 0.02)
    return {'x_prompt': x_prompt, 'x_sample': x_sample, 'c_prompt': c_prompt, 'c_sample': c_sample,
            'cache_kv_latent': cache_kv_latent, 'cache_k_rope': cache_k_rope, 'state_gla': state_gla,
            'page_table': page_table,
            'w_ada': w_ada, 'b_ada': b_ada, 'w_in': w_in, 'g_q_norm': g_q_norm, 'w_uq': w_uq,
            'g_kv_norm': g_kv_norm, 'w_uk': w_uk, 'w_uv': w_uv, 'w_gla_f2': w_gla_f2, 'b_gla_f': b_gla_f,
            'g_gla_norm': g_gla_norm, 'w_a_out': w_a_out, 'w_b_out': w_b_out, 'w_o': w_o,
            'ln1_g': ln1_g, 'ln1_b': ln1_b, 'w_router': w_router, 'router_bias': router_bias,
            'w_e_gate': w_e_gate, 'w_e_up': w_e_up, 'w_e_down': w_e_down,
            'w_s_gate': w_s_gate, 'w_s_up': w_s_up, 'w_s_down': w_s_down, 'ln2_g': ln2_g, 'ln2_b': ln2_b}


def reference(x_prompt, x_sample, c_prompt, c_sample, cache_kv_latent, cache_k_rope, state_gla, page_table,
              w_ada, b_ada, w_in, g_q_norm, w_uq, g_kv_norm, w_uk, w_uv, w_gla_f2, b_gla_f, g_gla_norm,
              w_a_out, w_b_out, w_o, ln1_g, ln1_b, w_router, router_bias, w_e_gate, w_e_up, w_e_down,
              w_s_gate, w_s_up, w_s_down, ln2_g, ln2_b):
    w_all = {'w_ada': w_ada, 'b_ada': b_ada, 'w_in': w_in, 'g_q_norm': g_q_norm, 'w_uq': w_uq,
             'g_kv_norm': g_kv_norm, 'w_uk': w_uk, 'w_uv': w_uv, 'w_gla_f2': w_gla_f2, 'b_gla_f': b_gla_f,
             'g_gla_norm': g_gla_norm, 'w_a_out': w_a_out, 'w_b_out': w_b_out, 'w_o': w_o,
             'ln1_g': ln1_g, 'ln1_b': ln1_b, 'w_router': w_router, 'router_bias': router_bias,
             'w_e_gate': w_e_gate, 'w_e_up': w_e_up, 'w_e_down': w_e_down,
             'w_s_gate': w_s_gate, 'w_s_up': w_s_up, 'w_s_down': w_s_down, 'ln2_g': ln2_g, 'ln2_b': ln2_b}
    n_prompt, s_prompt = x_prompt.shape[0], x_prompt.shape[1]
    s_sample = x_sample.shape[1]
    pos_p = jnp.arange(s_prompt, dtype=jnp.int32)
    pos_s = PAST_LEN + jnp.arange(s_sample, dtype=jnp.int32)
    yp, ys = x_prompt, x_sample
    kv_p, kr_p, st_p, kv_s, kr_s, st_s = [], [], [], [], [], []
    for layer in range(DEPTH):
        w = {name: arr[layer] for name, arr in w_all.items()}
        s0_p = jnp.zeros((n_prompt, H_A, DK_A, DV_A), jnp.float32)
        yp, ckv_p, krope_p, sfin_p = trunk_layer(yp, c_prompt, pos_p, s0_p, GLA_CHUNK, mla_prompt_attend, w)
        def attend_sample(ql, qr, ck, kr, layer=layer):
            return mla_sample_attend(ql, qr, ck, kr, cache_kv_latent, cache_k_rope, page_table, layer)
        ys, ckv_s, krope_s, sfin_s = trunk_layer(ys, c_sample, pos_s, state_gla[layer],
                                                 min(GLA_CHUNK, s_sample), attend_sample, w)
        kv_p.append(ckv_p)
        kr_p.append(krope_p)
        st_p.append(sfin_p.astype(state_gla.dtype))
        kv_s.append(ckv_s)
        kr_s.append(krope_s)
        st_s.append(sfin_s.astype(state_gla.dtype))
    return (yp, ys, jnp.stack(kv_p), jnp.stack(kr_p), jnp.stack(st_p), jnp.stack(kv_s), jnp.stack(kr_s), jnp.stack(st_s))
```

```python
import functools

import numpy as np
import jax
import jax.numpy as jnp
from jax import lax
from jax.experimental import pallas as pl
from jax.experimental.pallas import tpu as pltpu

F32 = jnp.float32
BF16 = jnp.bfloat16

D_MODEL = 1024
H_A, DK_A, DV_A, GLA_LR, GLA_TAU = 4, 128, 256, 16, 16.0
H_B, Q_LORA, KV_LORA, QK_NOPE, QK_ROPE, V_HEAD = 8, 384, 256, 128, 64, 128
ROPE_THETA = 10000.0
ATTN_SCALE = (QK_NOPE + QK_ROPE) ** -0.5
N_EXPERTS, TOP_K, N_GROUPS, TOPK_GROUPS = 256, 8, 8, 4
GROUP_SIZE = N_EXPERTS // N_GROUPS
D_EXPERT, D_SHARED, ROUTED_SCALE = 256, 256, 2.5
EPS = 1e-6
IN_SIZES = (Q_LORA, KV_LORA, QK_ROPE, H_A * DK_A, H_A * DK_A, H_A * DV_A, H_A * DV_A, GLA_LR, D_MODEL, D_MODEL)

LANES = 128
QCAT = KV_LORA + LANES
VMEM_LIMIT = 56 * 1024 * 1024

TOKEN_TILE = 256
FLASH_TQ, FLASH_TK = 256, 512
GLA_CHUNK_PROMPT = 64
MOE_ROWS = 128

NN = (((1,), (0,)), ((), ()))
NT = (((1,), (1,)), ((), ()))
TN = (((0,), (0,)), ((), ()))


def _dot(a, b, dims=NN):
    return lax.dot_general(a, b, dims, preferred_element_type=F32)


def _split3(x):
    x1 = x.astype(BF16)
    r1 = x - x1.astype(F32)
    x2 = r1.astype(BF16)
    x3 = (r1 - x2.astype(F32)).astype(BF16)
    return x1, x2, x3


def _dot_hi(a, b, dims=NN):
    a1, a2, a3 = _split3(a)
    b1, b2, b3 = _split3(b)
    small = _dot(a3, b1, dims) + _dot(a2, b2, dims) + _dot(a1, b3, dims)
    mid = _dot(a2, b1, dims) + _dot(a1, b2, dims)
    return (small + mid) + _dot(a1, b1, dims)


def _dot_hi_exact_lhs(w, x, dims=NN):
    x1, x2, x3 = _split3(x)
    return (_dot(w, x3, dims) + _dot(w, x2, dims)) + _dot(w, x1, dims)


def _sigmoid(x):
    return 1.0 / (1.0 + jnp.exp(-x))


def _silu(x):
    return x * _sigmoid(x)


def _rms(x, g):
    return x * lax.rsqrt(jnp.mean(x * x, axis=-1, keepdims=True) + EPS) * g


def _layer_norm(x, g, b):
    mu = jnp.mean(x, axis=-1, keepdims=True)
    xc = x - mu
    var = jnp.mean(xc * xc, axis=-1, keepdims=True)
    return xc * lax.rsqrt(var + EPS) * g + b


def _params(sem):
    return pltpu.CompilerParams(dimension_semantics=sem, vmem_limit_bytes=VMEM_LIMIT)


def _const_spec(shape):
    nd = len(shape)
    return pl.BlockSpec(shape, lambda *_: (0,) * nd)


def _adaln_kernel(c_ref, w_ref, b_ref, o_ref):
    o_ref[...] = _dot_hi(_silu(c_ref[...]), w_ref[...]) + b_ref[...]


def adaln(c, w_ada, b_ada):
    n, d = c.shape
    e = w_ada.shape[1]
    tn = 512
    return pl.pallas_call(
        _adaln_kernel,
        grid=(e // tn,),
        in_specs=[_const_spec((n, d)), pl.BlockSpec((d, tn), lambda j: (0, j)), pl.BlockSpec((1, tn), lambda j: (0, j))],
        out_specs=pl.BlockSpec((n, tn), lambda j: (0, j)),
        out_shape=jax.ShapeDtypeStruct((n, e), F32),
        compiler_params=_params(("arbitrary",)),
        name="adaln",
    )(c, w_ada, b_ada.reshape(1, e))


W1_COLS = Q_LORA + KV_LORA + 2 * LANES
WQ_COLS = H_B * QK_NOPE + 2 * H_B * LANES
WG_COLS = 2 * H_A * DK_A + H_A * DV_A + LANES


def _proj_kernel(x_ref, sc_ref, sh_ref, cos_ref, sin_ref, w1_ref, wq_ref, wuk_ref, wg_ref, wf2_ref, bf_ref,
                 gq_ref, gkv_ref,
                 ckv_ref, krope_ref, kcat_ref, qcat_ref, qg_ref, kg_ref, vg_ref, logf_ref):
    gb, rb, d = x_ref.shape
    tm = gb * rb
    h = (x_ref[...] * (1.0 + sc_ref[...]) + sh_ref[...]).reshape(tm, d).astype(BF16)
    cos = cos_ref[...]
    sin = sin_ref[...]

    z1 = _dot(h, w1_ref[...])
    ckv = _rms(z1[:, Q_LORA:Q_LORA + KV_LORA], gkv_ref[...])
    o_kr = Q_LORA + KV_LORA
    krope = z1[:, o_kr:o_kr + LANES] * cos + z1[:, o_kr + LANES:o_kr + 2 * LANES] * sin
    ckv_ref[...] = ckv
    krope_ref[...] = krope[:, :QK_ROPE]
    kcat_ref[:, :KV_LORA] = ckv.astype(BF16)
    kcat_ref[:, KV_LORA:] = krope.astype(BF16)

    qn = _rms(z1[:, :Q_LORA], gq_ref[...]).astype(BF16)
    q2 = _dot(qn, wq_ref[...])
    o_r = H_B * QK_NOPE
    o_s = o_r + H_B * LANES
    for hh in range(H_B):
        q_nope = q2[:, hh * QK_NOPE:(hh + 1) * QK_NOPE].astype(BF16)
        q_lat = _dot(q_nope, wuk_ref[hh])
        q_rope = q2[:, o_r + hh * LANES:o_r + (hh + 1) * LANES] * cos + q2[:, o_s + hh * LANES:o_s + (hh + 1) * LANES] * sin
        qcat_ref[:, hh * QCAT:hh * QCAT + KV_LORA] = (q_lat * ATTN_SCALE).astype(BF16)
        qcat_ref[:, hh * QCAT + KV_LORA:(hh + 1) * QCAT] = (q_rope * ATTN_SCALE).astype(BF16)

    z2 = _dot(h, wg_ref[...])
    nk = H_A * DK_A
    qg_ref[...] = z2[:, :nk] * DK_A ** -0.5
    kg_ref[...] = z2[:, nk:2 * nk]
    vg_ref[...] = z2[:, 2 * nk:2 * nk + H_A * DV_A]
    fa = z2[:, 2 * nk + H_A * DV_A:].astype(BF16)
    f_pre = _dot(fa, wf2_ref[...]) + bf_ref[...]
    log_sig = jnp.minimum(f_pre, 0.0) - jnp.log(1.0 + jnp.exp(-jnp.abs(f_pre)))
    logf_ref[...] = log_sig / GLA_TAU


def _group_tiling(g, r):
    if r >= TOKEN_TILE:
        assert r % TOKEN_TILE == 0
        return 1, TOKEN_TILE
    assert TOKEN_TILE % r == 0 and g % (TOKEN_TILE // r) == 0 and r % 8 == 0
    return TOKEN_TILE // r, r


def _group_specs(g, r, d):
    gb, rb = _group_tiling(g, r)
    nr = r // rb
    x_spec = pl.BlockSpec((gb, rb, d), lambda i: (i // nr, i % nr, 0))
    mod_spec = pl.BlockSpec((gb, 1, d), lambda i: (i // nr, 0, 0))
    return x_spec, mod_spec, (g // gb) * nr


def _tok_spec(cols):
    return pl.BlockSpec((TOKEN_TILE, cols), lambda i: (i, 0))


def input_projections(x, sc, sh, cos_tbl, sin_tbl, pw):
    g, r, d = x.shape
    t = g * r
    x_spec, mod_spec, n_tiles = _group_specs(g, r, d)
    n_tbl = cos_tbl.shape[0] // TOKEN_TILE
    tbl_spec = pl.BlockSpec((TOKEN_TILE, LANES), lambda i: (i % n_tbl, 0))
    nk = H_A * DK_A
    out_cols = [(KV_LORA, F32), (QK_ROPE, F32), (QCAT, BF16), (H_B * QCAT, BF16), (nk, F32), (nk, F32),
                (H_A * DV_A, F32), (nk, F32)]
    return pl.pallas_call(
        _proj_kernel,
        grid=(n_tiles,),
        in_specs=[x_spec, mod_spec, mod_spec, tbl_spec, tbl_spec,
                  _const_spec(pw["w1"].shape), _const_spec(pw["wq"].shape), _const_spec(pw["wuk"].shape),
                  _const_spec(pw["wg"].shape), _const_spec(pw["wf2"].shape), _const_spec(pw["bf"].shape),
                  _const_spec(pw["gq"].shape), _const_spec(pw["gkv"].shape)],
        out_specs=[_tok_spec(c) for c, _ in out_cols],
        out_shape=[jax.ShapeDtypeStruct((t, c), dt) for c, dt in out_cols],
        compiler_params=_params(("arbitrary",)),
        name="input_proj",
    )(x, sc, sh, cos_tbl, sin_tbl, pw["w1"], pw["wq"], pw["wuk"], pw["wg"], pw["wf2"], pw["bf"], pw["gq"], pw["gkv"])


def _flash_kernel(qi_ref, ki_ref, q_ref, k_ref, o_ref, m_sc, l_sc, acc_sc, *, tq, tk):
    step = pl.program_id(1)
    qi = qi_ref[step]
    ki = ki_ref[step]

    @pl.when(ki == 0)
    def _():
        m_sc[...] = jnp.full(m_sc.shape, -jnp.inf, F32)
        l_sc[...] = jnp.zeros(l_sc.shape, F32)
        acc_sc[...] = jnp.zeros(acc_sc.shape, F32)

    def update(masked):
        k = k_ref[...]
        v = k[:, :KV_LORA]
        if masked:
            row = qi * tq + lax.broadcasted_iota(jnp.int32, (tq, tk), 0)
            col = ki * tk + lax.broadcasted_iota(jnp.int32, (tq, tk), 1)
            visible = col <= row
        for hh in range(H_B):
            s = _dot(q_ref[:, hh * QCAT:(hh + 1) * QCAT], k, NT)
            if masked:
                s = jnp.where(visible, s, -jnp.inf)
            m_prev = m_sc[hh][:, :1]
            m_new = jnp.maximum(m_prev, jnp.max(s, axis=-1, keepdims=True))
            corr = jnp.exp(m_prev - m_new)
            p = jnp.exp(s - m_new)
            l_sc[hh] = jnp.broadcast_to(l_sc[hh][:, :1] * corr + jnp.sum(p, axis=-1, keepdims=True), (tq, LANES))
            acc_sc[hh] = acc_sc[hh] * corr + _dot(p.astype(BF16), v)
            m_sc[hh] = jnp.broadcast_to(m_new, (tq, LANES))

    crosses_diagonal = (ki + 1) * tk - 1 > qi * tq

    @pl.when(crosses_diagonal)
    def _():
        update(True)

    @pl.when(jnp.logical_not(crosses_diagonal))
    def _():
        update(False)

    @pl.when((ki + 1) * tk >= (qi + 1) * tq)
    def _():
        for hh in range(H_B):
            o_ref[:, hh * KV_LORA:(hh + 1) * KV_LORA] = (acc_sc[hh] / l_sc[hh][:, :1]).astype(o_ref.dtype)


def mla_prompt_attention(qcat, kcat, n_seq, seq):
    tq, tk = min(FLASH_TQ, seq), min(FLASH_TK, seq)
    assert seq % tq == 0 and seq % tk == 0
    nq, nkb = seq // tq, seq // tk
    qi_list, ki_list = [], []
    for qi in range(nq):
        for ki in range(-(-((qi + 1) * tq) // tk)):
            qi_list.append(qi)
            ki_list.append(ki)
    qi_tbl = jnp.asarray(np.array(qi_list, np.int32))
    ki_tbl = jnp.asarray(np.array(ki_list, np.int32))
    grid_spec = pltpu.PrefetchScalarGridSpec(
        num_scalar_prefetch=2,
        grid=(n_seq, len(qi_list)),
        in_specs=[pl.BlockSpec((tq, H_B * QCAT), lambda b, s, qt, kt: (b * nq + qt[s], 0)),
                  pl.BlockSpec((tk, QCAT), lambda b, s, qt, kt: (b * nkb + kt[s], 0))],
        out_specs=pl.BlockSpec((tq, H_B * KV_LORA), lambda b, s, qt, kt: (b * nq + qt[s], 0)),
        scratch_shapes=[pltpu.VMEM((H_B, tq, LANES), F32), pltpu.VMEM((H_B, tq, LANES), F32),
                        pltpu.VMEM((H_B, tq, KV_LORA), F32)],
    )
    return pl.pallas_call(
        functools.partial(_flash_kernel, tq=tq, tk=tk),
        grid_spec=grid_spec,
        out_shape=jax.ShapeDtypeStruct((n_seq * seq, H_B * KV_LORA), BF16),
        compiler_params=_params(("arbitrary", "arbitrary")),
        name="mla_prompt_attn",
    )(qi_tbl, ki_tbl, qcat, kcat)


def _paged_kernel(pt_ref, q_ref, knew_ref, ck_ref, kr_ref, o_ref, m_sc, l_sc, acc_sc, *, n_new):
    page = pl.program_id(1)
    rows = n_new * H_B

    @pl.when(page == 0)
    def _():
        q = q_ref[0].astype(F32)
        kn = knew_ref[0].astype(F32)
        tok = lax.broadcasted_iota(jnp.int32, (rows, 1), 0) // H_B
        cols = []
        for j in range(n_new):
            sj = jnp.sum(q * kn[j:j + 1, :], axis=-1, keepdims=True)
            cols.append(jnp.where(j <= tok, sj, -jnp.inf))
        m = cols[0]
        for j in range(1, n_new):
            m = jnp.maximum(m, cols[j])
        l = jnp.zeros((rows, 1), F32)
        acc = jnp.zeros((rows, KV_LORA), F32)
        for j in range(n_new):
            pj = jnp.exp(cols[j] - m)
            l = l + pj
            acc = acc + pj * kn[j:j + 1, :KV_LORA]
        m_sc[...] = jnp.broadcast_to(m, (rows, LANES))
        l_sc[...] = jnp.broadcast_to(l, (rows, LANES))
        acc_sc[...] = acc

    q = q_ref[0]
    ck = ck_ref[0, 0].astype(BF16)
    kr = kr_ref[0, 0].astype(BF16)
    s = _dot(q[:, :KV_LORA], ck, NT) + _dot(q[:, KV_LORA:KV_LORA + QK_ROPE], kr, NT)
    m_prev = m_sc[:, :1]
    m_new = jnp.maximum(m_prev, jnp.max(s, axis=-1, keepdims=True))
    corr = jnp.exp(m_prev - m_new)
    p = jnp.exp(s - m_new)
    l_new = l_sc[:, :1] * corr + jnp.sum(p, axis=-1, keepdims=True)
    acc = acc_sc[...] * corr + _dot(p.astype(BF16), ck)
    m_sc[...] = jnp.broadcast_to(m_new, (rows, LANES))
    l_sc[...] = jnp.broadcast_to(l_new, (rows, LANES))
    acc_sc[...] = acc

    @pl.when(page == pl.num_programs(1) - 1)
    def _():
        o_ref[0] = (acc / l_new).astype(o_ref.dtype)


def mla_sample_attention(qcat, kcat, cache_kv, cache_kr, page_table, layer, n_seq, n_new):
    n_pages = page_table.shape[1]
    page = cache_kv.shape[2]
    rows = n_new * H_B
    q3 = qcat.reshape(n_seq, rows, QCAT)
    k3 = kcat.reshape(n_seq, n_new, QCAT)
    grid_spec = pltpu.PrefetchScalarGridSpec(
        num_scalar_prefetch=1,
        grid=(n_seq, n_pages),
        in_specs=[pl.BlockSpec((1, rows, QCAT), lambda b, p, pt: (b, 0, 0)),
                  pl.BlockSpec((1, n_new, QCAT), lambda b, p, pt: (b, 0, 0)),
                  pl.BlockSpec((1, 1, page, KV_LORA), lambda b, p, pt: (layer, pt[b, p], 0, 0)),
                  pl.BlockSpec((1, 1, page, QK_ROPE), lambda b, p, pt: (layer, pt[b, p], 0, 0))],
        out_specs=pl.BlockSpec((1, rows, KV_LORA), lambda b, p, pt: (b, 0, 0)),
        scratch_shapes=[pltpu.VMEM((rows, LANES), F32), pltpu.VMEM((rows, LANES), F32), pltpu.VMEM((rows, KV_LORA), F32)],
    )
    o = pl.pallas_call(
        functools.partial(_paged_kernel, n_new=n_new),
        grid_spec=grid_spec,
        out_shape=jax.ShapeDtypeStruct((n_seq, rows, KV_LORA), BF16),
        compiler_params=_params(("arbitrary", "arbitrary")),
        name="mla_sample_attn",
    )(page_table, q3, k3, cache_kv, cache_kr)
    return o.reshape(n_seq * n_new, H_B * KV_LORA)


def _gla_cumsum_matrix(chunk):
    blocks = [np.tril(np.ones((chunk, chunk), np.float32))]
    m = chunk // 2
    while m >= 1:
        w = np.zeros((chunk, chunk), np.float32)
        for t in range(chunk):
            seg = (t // m) * m
            if (t % (2 * m)) >= m:
                w[t, seg:t + 1] = 1.0
            else:
                w[t, t + 1:seg + m] = 1.0
        blocks.append(w)
        m //= 2
    return np.concatenate(blocks, axis=0)


def _gla_kernel(q_ref, k_ref, v_ref, g_ref, wc_ref, s0_ref, o_ref, sfin_ref, s_sc, *, chunk):
    c_idx = pl.program_id(1)

    @pl.when(c_idx == 0)
    def _():
        s_sc[...] = s0_ref[0]

    n_lev = chunk.bit_length() - 1
    row = lax.broadcasted_iota(jnp.int32, (chunk, chunk), 0)
    col = lax.broadcasted_iota(jnp.int32, (chunk, chunk), 1)
    row_in_chunk = lax.broadcasted_iota(jnp.int32, (chunk, DK_A), 0)
    wc = wc_ref[...]
    for hh in range(H_A):
        q = q_ref[:, hh * DK_A:(hh + 1) * DK_A]
        k = k_ref[:, hh * DK_A:(hh + 1) * DK_A]
        v = v_ref[:, hh * DV_A:(hh + 1) * DV_A].astype(BF16)
        g = g_ref[:, hh * DK_A:(hh + 1) * DK_A]
        sums = _dot_hi_exact_lhs(wc, g)
        b = sums[:chunk]
        b_last = b[chunk - 1:chunk, :]
        s_prev = s_sc[hh]

        a = jnp.where(row == col, _dot(q.astype(BF16), k.astype(BF16), NT), 0.0)
        for lev in range(n_lev):
            m = chunk >> (lev + 1)
            scale = jnp.exp(sums[(lev + 1) * chunk:(lev + 2) * chunk])
            upper = (row_in_chunk % (2 * m)) >= m
            q_l = jnp.where(upper, q * scale, 0.0).astype(BF16)
            k_l = jnp.where(upper, 0.0, k * scale).astype(BF16)
            same_block = (row // (2 * m)) == (col // (2 * m))
            a = a + jnp.where(same_block, _dot(q_l, k_l, NT), 0.0)

        inter = _dot((q * jnp.exp(b)).astype(BF16), s_prev.astype(BF16))
        o_ref[:, hh * DV_A:(hh + 1) * DV_A] = inter + _dot(a.astype(BF16), v)

        k_dec = (k * jnp.exp(b_last - b)).astype(BF16)
        decay_col = jnp.transpose(jnp.broadcast_to(jnp.exp(b_last), (DK_A, DK_A)))[:, :1]
        s_sc[hh] = decay_col * s_prev + _dot(k_dec, v, TN)

    @pl.when(c_idx == pl.num_programs(1) - 1)
    def _():
        sfin_ref[0] = s_sc[...]


def gla(qg, kg, vg, logf, s0, n_seq, seq, chunk):
    assert seq % chunk == 0 and chunk & (chunk - 1) == 0 and chunk % 8 == 0
    n_chunks = seq // chunk
    wc = jnp.asarray(_gla_cumsum_matrix(chunk), BF16)
    nk = H_A * DK_A

    def tok(cols):
        return pl.BlockSpec((chunk, cols), lambda b, c: (b * n_chunks + c, 0))

    state_spec = pl.BlockSpec((1, H_A, DK_A, DV_A), lambda b, c: (b, 0, 0, 0))
    return pl.pallas_call(
        functools.partial(_gla_kernel, chunk=chunk),
        grid=(n_seq, n_chunks),
        in_specs=[tok(nk), tok(nk), tok(H_A * DV_A), tok(nk), _const_spec(wc.shape), state_spec],
        out_specs=[tok(H_A * DV_A), state_spec],
        out_shape=[jax.ShapeDtypeStruct((n_seq * seq, H_A * DV_A), F32),
                   jax.ShapeDtypeStruct((n_seq, H_A, DK_A, DV_A), F32)],
        scratch_shapes=[pltpu.VMEM((H_A, DK_A, DV_A), F32)],
        compiler_params=_params(("arbitrary", "arbitrary")),
        name="gla",
    )(qg, kg, vg, logf, wc, s0)


def _route(h2, wrt_ref, rbias_ref, tm):
    logits = _dot_hi(wrt_ref[...], h2, NT)
    scores = _sigmoid(logits)
    sel = scores + rbias_ref[...]
    neg = -jnp.inf

    iota_g = lax.broadcasted_iota(jnp.int32, (GROUP_SIZE, tm), 0).astype(F32)
    g_rows = []
    for g in range(N_GROUPS):
        blk = sel[g * GROUP_SIZE:(g + 1) * GROUP_SIZE]
        m1 = jnp.max(blk, axis=0, keepdims=True)
        first = jnp.min(jnp.where(blk == m1, iota_g, float(GROUP_SIZE)), axis=0, keepdims=True)
        m2 = jnp.max(jnp.where(iota_g == first, neg, blk), axis=0, keepdims=True)
        g_rows.append(m1 + m2)
    g_score = jnp.concatenate(g_rows, axis=0)

    iota_n = lax.broadcasted_iota(jnp.int32, (N_GROUPS, tm), 0).astype(F32)
    g_keep = jnp.zeros((N_GROUPS, tm), F32)
    for _ in range(TOPK_GROUPS):
        mx = jnp.max(g_score, axis=0, keepdims=True)
        first = jnp.min(jnp.where(g_score == mx, iota_n, float(N_GROUPS)), axis=0, keepdims=True)
        hit = iota_n == first
        g_keep = jnp.where(hit, 1.0, g_keep)
        g_score = jnp.where(hit, neg, g_score)

    sel_m = jnp.concatenate(
        [jnp.where(g_keep[g:g + 1] > 0.0, sel[g * GROUP_SIZE:(g + 1) * GROUP_SIZE], neg) for g in range(N_GROUPS)], axis=0)

    iota_e = lax.broadcasted_iota(jnp.int32, (N_EXPERTS, tm), 0).astype(F32)
    ids, wts = [], []
    for _ in range(TOP_K):
        mx = jnp.max(sel_m, axis=0, keepdims=True)
        first = jnp.min(jnp.where(sel_m == mx, iota_e, float(N_EXPERTS)), axis=0, keepdims=True)
        hit = iota_e == first
        ids.append(first)
        wts.append(jnp.sum(jnp.where(hit, scores, 0.0), axis=0, keepdims=True))
        sel_m = jnp.where(hit, neg, sel_m)
    ids = jnp.concatenate(ids, axis=0)
    wts = jnp.concatenate(wts, axis=0)
    wts = wts / jnp.sum(wts, axis=0, keepdims=True) * ROUTED_SCALE
    return ids, wts


def _merge_kernel(x_ref, scm_ref, shm_ref, gm_ref, scf_ref, shf_ref, olat_ref, ogla_ref,
                  w3_ref, wuv_ref, wb_ref, wa_ref, wo_ref, ggla_ref, ln1g_ref, ln1b_ref, wrt_ref, rbias_ref,
                  x1_ref, h2_ref, ids_ref, wts_ref, *, alpha):
    gb, rb, d = x_ref.shape
    tm = gb * rb
    x = x_ref[...]
    h = (x * (1.0 + scm_ref[...]) + shm_ref[...]).reshape(tm, d).astype(BF16)
    z3 = _dot(h, w3_ref[...])

    vb = [_dot(olat_ref[:, hh * KV_LORA:(hh + 1) * KV_LORA], wuv_ref[hh]).astype(BF16) for hh in range(H_B)]
    y_b = _dot(jnp.concatenate(vb, axis=-1), wb_ref[...])

    ga = []
    for hh in range(H_A):
        o_n = _rms(ogla_ref[:, hh * DV_A:(hh + 1) * DV_A], ggla_ref[...])
        ga.append((o_n * _silu(z3[:, hh * DV_A:(hh + 1) * DV_A])).astype(BF16))
    y_a = _dot(jnp.concatenate(ga, axis=-1), wa_ref[...])

    merged = _sigmoid(z3[:, d:2 * d]) * y_a + _sigmoid(z3[:, 2 * d:]) * y_b
    mix = _dot(merged.astype(BF16), wo_ref[...]).reshape(gb, rb, d)
    x1 = _layer_norm(alpha * x + gm_ref[...] * mix, ln1g_ref[...], ln1b_ref[...])
    x1_ref[...] = x1
    h2 = (x1 * (1.0 + scf_ref[...]) + shf_ref[...]).reshape(tm, d)
    h2_ref[...] = h2.astype(BF16)
    ids, wts = _route(h2, wrt_ref, rbias_ref, tm)
    ids_ref[...] = ids.astype(jnp.int32)
    wts_ref[...] = wts


def merge_and_route(x, mods, olat, ogla, mw, alpha):
    g, r, d = x.shape
    t = g * r
    x_spec, mod_spec, n_tiles = _group_specs(g, r, d)
    names = ["w3", "wuv", "wb", "wa", "wo", "ggla", "ln1g", "ln1b", "wrt", "rbias"]
    kt_spec = pl.BlockSpec((TOP_K, TOKEN_TILE), lambda i: (0, i))
    return pl.pallas_call(
        functools.partial(_merge_kernel, alpha=alpha),
        grid=(n_tiles,),
        in_specs=[x_spec] + [mod_spec] * 5 + [_tok_spec(H_B * KV_LORA), _tok_spec(H_A * DV_A)]
                 + [_const_spec(mw[n].shape) for n in names],
        out_specs=[x_spec, _tok_spec(d), kt_spec, kt_spec],
        out_shape=[jax.ShapeDtypeStruct((g, r, d), F32), jax.ShapeDtypeStruct((t, d), BF16),
                   jax.ShapeDtypeStruct((TOP_K, t), jnp.int32), jax.ShapeDtypeStruct((TOP_K, t), F32)],
        compiler_params=_params(("arbitrary",)),
        name="merge_route",
    )(x, *mods, olat, ogla, *[mw[n] for n in names])


def _moe_kernel(blk_e_ref, n_used_ref, x_ref, wg_ref, wu_ref, wd_ref, y_ref):
    i = pl.program_id(0)

    @pl.when(i < n_used_ref[0])
    def _():
        x = x_ref[...]
        gate = _dot(x, wg_ref[0].astype(BF16))
        up = _dot(x, wu_ref[0].astype(BF16))
        act = (_silu(gate) * up).astype(BF16)
        y_ref[...] = _dot(act, wd_ref[0].astype(BF16)).astype(y_ref.dtype)

    @pl.when(i >= n_used_ref[0])
    def _():
        y_ref[...] = jnp.zeros(y_ref.shape, y_ref.dtype)


def routed_expert_blocks(x_sorted, blk_e, n_used, w_gate, w_up, w_down):
    m_pad, d = x_sorted.shape
    n_blk = m_pad // MOE_ROWS
    de = w_gate.shape[-1]
    grid_spec = pltpu.PrefetchScalarGridSpec(
        num_scalar_prefetch=2,
        grid=(n_blk,),
        in_specs=[pl.BlockSpec((MOE_ROWS, d), lambda i, be, nu: (i, 0)),
                  pl.BlockSpec((1, d, de), lambda i, be, nu: (be[i], 0, 0)),
                  pl.BlockSpec((1, d, de), lambda i, be, nu: (be[i], 0, 0)),
                  pl.BlockSpec((1, de, d), lambda i, be, nu: (be[i], 0, 0))],
        out_specs=pl.BlockSpec((MOE_ROWS, d), lambda i, be, nu: (i, 0)),
    )
    return pl.pallas_call(
        _moe_kernel,
        grid_spec=grid_spec,
        out_shape=jax.ShapeDtypeStruct((m_pad, d), F32),
        compiler_params=_params(("arbitrary",)),
        name="moe_experts",
    )(blk_e, n_used, x_sorted, w_gate, w_up, w_down)


def dispatch(ids):
    k, t = ids.shape
    m = k * t
    flat_e = ids.reshape(m)
    order = jnp.argsort(flat_e)
    se = flat_e[order]
    counts = jnp.bincount(flat_e, length=N_EXPERTS)
    padded = (counts + MOE_ROWS - 1) // MOE_ROWS * MOE_ROWS
    pend = jnp.cumsum(padded)
    pstart = pend - padded
    ustart = jnp.cumsum(counts) - counts
    pos_sorted = (pstart[se] + jnp.arange(m) - ustart[se]).astype(jnp.int32)
    m_pad = -(-(m + N_EXPERTS * (MOE_ROWS - 1)) // MOE_ROWS) * MOE_ROWS
    n_blk = m_pad // MOE_ROWS
    tok_pad = jnp.zeros((m_pad,), jnp.int32).at[pos_sorted].set((order % t).astype(jnp.int32))
    pos = jnp.zeros((m,), jnp.int32).at[order].set(pos_sorted).reshape(k, t)
    blk_e = jnp.minimum(jnp.searchsorted(pend, jnp.arange(n_blk) * MOE_ROWS, side="right"), N_EXPERTS - 1).astype(jnp.int32)
    n_used = (pend[-1] // MOE_ROWS).astype(jnp.int32).reshape(1)
    return tok_pad, pos, blk_e, n_used


def _final_kernel(x1_ref, gf_ref, h2_ref, yg_ref, wts_ref, wsg_ref, wsu_ref, wsd_ref, ln2g_ref, ln2b_ref, y_ref, *, alpha):
    gb, rb, d = x1_ref.shape
    h2 = h2_ref[...]
    act = (_silu(_dot(h2, wsg_ref[...])) * _dot(h2, wsu_ref[...])).astype(BF16)
    ffn = _dot(act, wsd_ref[...])
    w = wts_ref[...]
    routed = yg_ref[0] * w[:, 0:1]
    for j in range(1, TOP_K):
        routed = routed + yg_ref[j] * w[:, j:j + 1]
    ffn = (ffn + routed).reshape(gb, rb, d)
    y_ref[...] = _layer_norm(alpha * x1_ref[...] + gf_ref[...] * ffn, ln2g_ref[...], ln2b_ref[...])


def shared_combine_norm(x1, gf, h2, yg, wts_t, fw, alpha):
    g, r, d = x1.shape
    x_spec, mod_spec, n_tiles = _group_specs(g, r, d)
    names = ["wsg", "wsu", "wsd", "ln2g", "ln2b"]
    return pl.pallas_call(
        functools.partial(_final_kernel, alpha=alpha),
        grid=(n_tiles,),
        in_specs=[x_spec, mod_spec, _tok_spec(d), pl.BlockSpec((TOP_K, TOKEN_TILE, d), lambda i: (0, i, 0)),
                  _tok_spec(TOP_K)] + [_const_spec(fw[n].shape) for n in names],
        out_specs=x_spec,
        out_shape=jax.ShapeDtypeStruct((g, r, d), F32),
        compiler_params=_params(("arbitrary",)),
        name="shared_combine_norm",
    )(x1, gf, h2, yg, wts_t, *[fw[n] for n in names])


def _rot_cols(w):
    half = w.shape[-1] // 2
    return jnp.concatenate([-w[..., half:], w[..., :half]], axis=-1)


def _pad_cols(w, width):
    return jnp.pad(w, [(0, 0)] * (w.ndim - 1) + [(0, width - w.shape[-1])])


def _prepare_weights(w):
    cuts = np.cumsum((0,) + IN_SIZES)
    part = [w["w_in"][:, cuts[i]:cuts[i + 1]] for i in range(len(IN_SIZES))]
    w_qa, w_kva, w_kr, w_gq, w_gk, w_gv, w_go, w_gf, w_ga, w_gb = part
    w1 = jnp.concatenate([w_qa, w_kva, _pad_cols(w_kr, LANES), _pad_cols(_rot_cols(w_kr), LANES)], axis=1)
    wuq = w["w_uq"].reshape(Q_LORA, H_B, QK_NOPE + QK_ROPE)
    wuq_rope = wuq[:, :, QK_NOPE:]
    wq = jnp.concatenate([wuq[:, :, :QK_NOPE].reshape(Q_LORA, H_B * QK_NOPE),
                          _pad_cols(wuq_rope, LANES).reshape(Q_LORA, H_B * LANES),
                          _pad_cols(_rot_cols(wuq_rope), LANES).reshape(Q_LORA, H_B * LANES)], axis=1)
    wg = jnp.concatenate([w_gq, w_gk, w_gv, _pad_cols(w_gf, LANES)], axis=1)
    proj = dict(
        w1=w1.astype(BF16), wq=wq.astype(BF16), wg=wg.astype(BF16),
        wuk=jnp.transpose(w["w_uk"], (1, 2, 0)).astype(BF16),
        wf2=jnp.pad(w["w_gla_f2"], ((0, LANES - GLA_LR), (0, 0))).astype(BF16),
        bf=w["b_gla_f"].reshape(1, -1), gq=w["g_q_norm"].reshape(1, -1), gkv=w["g_kv_norm"].reshape(1, -1))
    merge = dict(
        w3=jnp.concatenate([w_go, w_ga, w_gb], axis=1).astype(BF16),
        wuv=jnp.transpose(w["w_uv"], (1, 0, 2)).astype(BF16),
        wb=w["w_b_out"].astype(BF16), wa=w["w_a_out"].astype(BF16), wo=w["w_o"].astype(BF16),
        ggla=w["g_gla_norm"].reshape(1, -1), ln1g=w["ln1_g"].reshape(1, -1), ln1b=w["ln1_b"].reshape(1, -1),
        wrt=jnp.transpose(w["w_router"]), rbias=w["router_bias"].reshape(-1, 1))
    final = dict(
        wsg=w["w_s_gate"].astype(BF16), wsu=w["w_s_up"].astype(BF16), wsd=w["w_s_down"].astype(BF16),
        ln2g=w["ln2_g"].reshape(1, -1), ln2b=w["ln2_b"].reshape(1, -1))
    return proj, merge, final


def _rope_tables(pos, tile_rows):
    half = QK_ROPE // 2
    freqs = ROPE_THETA ** (-jnp.arange(half, dtype=F32) / half)
    ang = pos.astype(F32)[:, None] * freqs
    cos = _pad_cols(jnp.concatenate([jnp.cos(ang)] * 2, axis=-1), LANES)
    sin = _pad_cols(jnp.concatenate([jnp.sin(ang)] * 2, axis=-1), LANES)
    if pos.shape[0] < tile_rows:
        rep = tile_rows // pos.shape[0]
        cos, sin = jnp.tile(cos, (rep, 1)), jnp.tile(sin, (rep, 1))
    return cos, sin


def _split_mods(mod, n):
    return [m.reshape(n, 1, D_MODEL) for m in jnp.split(mod, 6, axis=-1)]


def kernel(x_prompt, x_sample, c_prompt, c_sample, cache_kv_latent, cache_k_rope, state_gla, page_table, w_ada, b_ada, w_in, g_q_norm, w_uq, g_kv_norm, w_uk, w_uv, w_gla_f2, b_gla_f, g_gla_norm, w_a_out, w_b_out, w_o, ln1_g, ln1_b, w_router, router_bias, w_e_gate, w_e_up, w_e_down, w_s_gate, w_s_up, w_s_down, ln2_g, ln2_b):
    w_all = dict(w_ada=w_ada, b_ada=b_ada, w_in=w_in, g_q_norm=g_q_norm, w_uq=w_uq, g_kv_norm=g_kv_norm, w_uk=w_uk,
                 w_uv=w_uv, w_gla_f2=w_gla_f2, b_gla_f=b_gla_f, g_gla_norm=g_gla_norm, w_a_out=w_a_out,
                 w_b_out=w_b_out, w_o=w_o, ln1_g=ln1_g, ln1_b=ln1_b, w_router=w_router, router_bias=router_bias,
                 w_e_gate=w_e_gate, w_e_up=w_e_up, w_e_down=w_e_down, w_s_gate=w_s_gate, w_s_up=w_s_up,
                 w_s_down=w_s_down, ln2_g=ln2_g, ln2_b=ln2_b)
    depth = w_ada.shape[0]
    alpha = (2.0 * depth) ** 0.25
    n_p, s_p, _ = x_prompt.shape
    n_s, s_s, _ = x_sample.shape
    t_p, t_s = n_p * s_p, n_s * s_s
    past_len = page_table.shape[1] * cache_kv_latent.shape[2]
    cos_p, sin_p = _rope_tables(jnp.arange(s_p, dtype=jnp.int32), TOKEN_TILE)
    cos_s, sin_s = _rope_tables(past_len + jnp.arange(s_s, dtype=jnp.int32), TOKEN_TILE)

    n_c = n_p + n_s
    n_c_pad = -(-n_c // 16) * 16
    c_all = jnp.pad(jnp.concatenate([c_prompt, c_sample], axis=0), ((0, n_c_pad - n_c), (0, 0)))

    yp, ys = x_prompt, x_sample
    outs = [[] for _ in range(6)]
    for layer in range(depth):
        w = {name: arr[layer] for name, arr in w_all.items()}
        pw, mw, fw = _prepare_weights(w)
        mod = adaln(c_all, w["w_ada"], w["b_ada"])
        mods_p = _split_mods(mod[:n_p], n_p)
        mods_s = _split_mods(mod[n_p:n_c], n_s)

        ckv_p, kr_p, kcat_p, qcat_p, qg_p, kg_p, vg_p, lf_p = input_projections(yp, mods_p[1], mods_p[0], cos_p, sin_p, pw)
        olat_p = mla_prompt_attention(qcat_p, kcat_p, n_p, s_p)
        s0_p = jnp.zeros((n_p, H_A, DK_A, DV_A), F32)
        ogla_p, sfin_p = gla(qg_p, kg_p, vg_p, lf_p, s0_p, n_p, s_p, min(GLA_CHUNK_PROMPT, s_p))
        x1_p, h2_p, ids_p, wts_p = merge_and_route(
            yp, [mods_p[1], mods_p[0], mods_p[2], mods_p[4], mods_p[3]], olat_p, ogla_p, mw, alpha)

        ckv_s, kr_s, kcat_s, qcat_s, qg_s, kg_s, vg_s, lf_s = input_projections(ys, mods_s[1], mods_s[0], cos_s, sin_s, pw)
        olat_s = mla_sample_attention(qcat_s, kcat_s, cache_kv_latent, cache_k_rope, page_table, layer, n_s, s_s)
        ogla_s, sfin_s = gla(qg_s, kg_s, vg_s, lf_s, state_gla[layer], n_s, s_s, s_s)
        x1_s, h2_s, ids_s, wts_s = merge_and_route(
            ys, [mods_s[1], mods_s[0], mods_s[2], mods_s[4], mods_s[3]], olat_s, ogla_s, mw, alpha)

        h2 = jnp.concatenate([h2_p, h2_s], axis=0)
        ids = jnp.concatenate([ids_p, ids_s], axis=1)
        tok_pad, pos, blk_e, n_used = dispatch(ids)
        y_sorted = routed_expert_blocks(jnp.take(h2, tok_pad, axis=0), blk_e, n_used, w["w_e_gate"], w["w_e_up"], w["w_e_down"])
        yg = jnp.take(y_sorted, pos, axis=0)
        wts_t = jnp.transpose(jnp.concatenate([wts_p, wts_s], axis=1))

        yp = shared_combine_norm(x1_p, mods_p[5], h2_p, yg[:, :t_p], wts_t[:t_p], fw, alpha)
        ys = shared_combine_norm(x1_s, mods_s[5], h2_s, yg[:, t_p:], wts_t[t_p:], fw, alpha)

        for lst, val in zip(outs, [ckv_p.reshape(n_p, s_p, KV_LORA), kr_p.reshape(n_p, s_p, QK_ROPE), sfin_p,
                                   ckv_s.reshape(n_s, s_s, KV_LORA), kr_s.reshape(n_s, s_s, QK_ROPE), sfin_s]):
            lst.append(val.astype(state_gla.dtype) if val.ndim == 4 else val)
    return (yp, ys) + tuple(jnp.stack(o) for o in outs)
```

```python
import functools

import numpy as np
import jax
import jax.numpy as jnp
from jax import lax
from jax.experimental import pallas as pl
from jax.experimental.pallas import tpu as pltpu

F32 = jnp.float32
BF16 = jnp.bfloat16

D_MODEL = 1024
H_A, DK_A, DV_A, GLA_LR, GLA_TAU = 4, 128, 256, 16, 16.0
H_B, Q_LORA, KV_LORA, QK_NOPE, QK_ROPE, V_HEAD = 8, 384, 256, 128, 64, 128
ROPE_THETA = 10000.0
ATTN_SCALE = (QK_NOPE + QK_ROPE) ** -0.5
Q_SCALE = ATTN_SCALE * 1.4426950408889634
N_EXPERTS, TOP_K, N_GROUPS, TOPK_GROUPS = 256, 8, 8, 4
GROUP_SIZE = N_EXPERTS // N_GROUPS
D_EXPERT, D_SHARED, ROUTED_SCALE = 256, 256, 2.5
EPS = 1e-6
IN_SIZES = (Q_LORA, KV_LORA, QK_ROPE, H_A * DK_A, H_A * DK_A, H_A * DV_A, H_A * DV_A, GLA_LR, D_MODEL, D_MODEL)

LANES = 128
QCAT = KV_LORA + LANES
VMEM_LIMIT = 56 * 1024 * 1024

TOKEN_TILE = 256
FLASH_TQ, FLASH_TK = 256, 512
FLASH_RB = 1024
GLA_CHUNK_PROMPT = 64
MOE_ROWS = 128

NN = (((1,), (0,)), ((), ()))
NT = (((1,), (1,)), ((), ()))
TN = (((0,), (0,)), ((), ()))


def _dot(a, b, dims=NN):
    return lax.dot_general(a, b, dims, preferred_element_type=F32)


def _split3(x):
    x1 = x.astype(BF16)
    r1 = x - x1.astype(F32)
    x2 = r1.astype(BF16)
    x3 = (r1 - x2.astype(F32)).astype(BF16)
    return x1, x2, x3


def _dot_hi(a, b, dims=NN):
    a1, a2, a3 = _split3(a)
    b1, b2, b3 = _split3(b)
    small = _dot(a3, b1, dims) + _dot(a2, b2, dims) + _dot(a1, b3, dims)
    mid = _dot(a2, b1, dims) + _dot(a1, b2, dims)
    return (small + mid) + _dot(a1, b1, dims)


def _dot_hi_exact_lhs(w, x, dims=NN):
    x1, x2, x3 = _split3(x)
    return (_dot(w, x3, dims) + _dot(w, x2, dims)) + _dot(w, x1, dims)


def _sigmoid(x):
    return 1.0 / (1.0 + jnp.exp(-x))


def _silu(x):
    return x * _sigmoid(x)


def _rms(x, g):
    return x * lax.rsqrt(jnp.mean(x * x, axis=-1, keepdims=True) + EPS) * g


def _layer_norm(x, g, b):
    mu = jnp.mean(x, axis=-1, keepdims=True)
    xc = x - mu
    var = jnp.mean(xc * xc, axis=-1, keepdims=True)
    return xc * lax.rsqrt(var + EPS) * g + b


def _params(sem):
    return pltpu.CompilerParams(dimension_semantics=sem, vmem_limit_bytes=VMEM_LIMIT)


def _const_spec(shape):
    nd = len(shape)
    return pl.BlockSpec(shape, lambda *_: (0,) * nd)


def _adaln_kernel(c_ref, w_ref, b_ref, o_ref):
    o_ref[...] = _dot_hi(_silu(c_ref[...]), w_ref[...]) + b_ref[...]


def adaln(c, w_ada, b_ada):
    n, d = c.shape
    e = w_ada.shape[1]
    tn = 512
    return pl.pallas_call(
        _adaln_kernel,
        grid=(e // tn,),
        in_specs=[_const_spec((n, d)), pl.BlockSpec((d, tn), lambda j: (0, j)), pl.BlockSpec((1, tn), lambda j: (0, j))],
        out_specs=pl.BlockSpec((n, tn), lambda j: (0, j)),
        out_shape=jax.ShapeDtypeStruct((n, e), F32),
        compiler_params=_params(("arbitrary",)),
        name="adaln",
    )(c, w_ada, b_ada.reshape(1, e))


W1_COLS = Q_LORA + KV_LORA + 2 * LANES
WQ_COLS = H_B * QK_NOPE + 2 * H_B * LANES
WG_COLS = 2 * H_A * DK_A + H_A * DV_A + LANES


def _proj_kernel(x_ref, sc_ref, sh_ref, cos_ref, sin_ref, w1_ref, wq_ref, wuk_ref, wg_ref, wf2_ref, bf_ref,
                 gq_ref, gkv_ref,
                 ckv_ref, krope_ref, kcat_ref, qcat_ref, qg_ref, kg_ref, vg_ref, logf_ref):
    gb, rb, d = x_ref.shape
    tm = gb * rb
    h = (x_ref[...] * (1.0 + sc_ref[...]) + sh_ref[...]).reshape(tm, d).astype(BF16)
    cos = cos_ref[...]
    sin = sin_ref[...]

    z1 = _dot(h, w1_ref[...])
    ckv = _rms(z1[:, Q_LORA:Q_LORA + KV_LORA], gkv_ref[...])
    o_kr = Q_LORA + KV_LORA
    krope = z1[:, o_kr:o_kr + LANES] * cos + z1[:, o_kr + LANES:o_kr + 2 * LANES] * sin
    ckv_ref[...] = ckv
    krope_ref[...] = krope[:, :QK_ROPE]
    kcat_ref[:, :KV_LORA] = ckv.astype(BF16)
    kcat_ref[:, KV_LORA:] = krope.astype(BF16)

    qn = _rms(z1[:, :Q_LORA], gq_ref[...]).astype(BF16)
    q2 = _dot(qn, wq_ref[...])
    o_r = H_B * QK_NOPE
    o_s = o_r + H_B * LANES
    for hh in range(H_B):
        q_nope = q2[:, hh * QK_NOPE:(hh + 1) * QK_NOPE].astype(BF16)
        q_lat = _dot(q_nope, wuk_ref[hh])
        q_rope = q2[:, o_r + hh * LANES:o_r + (hh + 1) * LANES] * cos + q2[:, o_s + hh * LANES:o_s + (hh + 1) * LANES] * sin
        qcat_ref[:, hh * QCAT:hh * QCAT + KV_LORA] = (q_lat * Q_SCALE).astype(BF16)
        qcat_ref[:, hh * QCAT + KV_LORA:(hh + 1) * QCAT] = (q_rope * Q_SCALE).astype(BF16)

    z2 = _dot(h, wg_ref[...])
    nk = H_A * DK_A
    qg_ref[...] = z2[:, :nk] * DK_A ** -0.5
    kg_ref[...] = z2[:, nk:2 * nk]
    vg_ref[...] = z2[:, 2 * nk:2 * nk + H_A * DV_A]
    fa = z2[:, 2 * nk + H_A * DV_A:].astype(BF16)
    f_pre = _dot(fa, wf2_ref[...]) + bf_ref[...]
    log_sig = jnp.minimum(f_pre, 0.0) - jnp.log(1.0 + jnp.exp(-jnp.abs(f_pre)))
    logf_ref[...] = log_sig / GLA_TAU


def _group_tiling(g, r):
    if r >= TOKEN_TILE:
        assert r % TOKEN_TILE == 0
        return 1, TOKEN_TILE
    assert TOKEN_TILE % r == 0 and g % (TOKEN_TILE // r) == 0 and r % 8 == 0
    return TOKEN_TILE // r, r


def _group_specs(g, r, d):
    gb, rb = _group_tiling(g, r)
    nr = r // rb
    x_spec = pl.BlockSpec((gb, rb, d), lambda i: (i // nr, i % nr, 0))
    mod_spec = pl.BlockSpec((gb, 1, d), lambda i: (i // nr, 0, 0))
    return x_spec, mod_spec, (g // gb) * nr


def _tok_spec(cols):
    return pl.BlockSpec((TOKEN_TILE, cols), lambda i: (i, 0))


def input_projections(x, sc, sh, cos_tbl, sin_tbl, pw):
    g, r, d = x.shape
    t = g * r
    x_spec, mod_spec, n_tiles = _group_specs(g, r, d)
    n_tbl = cos_tbl.shape[0] // TOKEN_TILE
    tbl_spec = pl.BlockSpec((TOKEN_TILE, LANES), lambda i: (i % n_tbl, 0))
    nk = H_A * DK_A
    out_cols = [(KV_LORA, F32), (QK_ROPE, F32), (QCAT, BF16), (H_B * QCAT, BF16), (nk, F32), (nk, F32),
                (H_A * DV_A, F32), (nk, F32)]
    return pl.pallas_call(
        _proj_kernel,
        grid=(n_tiles,),
        in_specs=[x_spec, mod_spec, mod_spec, tbl_spec, tbl_spec,
                  _const_spec(pw["w1"].shape), _const_spec(pw["wq"].shape), _const_spec(pw["wuk"].shape),
                  _const_spec(pw["wg"].shape), _const_spec(pw["wf2"].shape), _const_spec(pw["bf"].shape),
                  _const_spec(pw["gq"].shape), _const_spec(pw["gkv"].shape)],
        out_specs=[_tok_spec(c) for c, _ in out_cols],
        out_shape=[jax.ShapeDtypeStruct((t, c), dt) for c, dt in out_cols],
        compiler_params=_params(("arbitrary",)),
        name="input_proj",
    )(x, sc, sh, cos_tbl, sin_tbl, pw["w1"], pw["wq"], pw["wuk"], pw["wg"], pw["wf2"], pw["bf"], pw["gq"], pw["gkv"])


def _flash_kernel(qi_ref, ki_ref, q_ref, k_ref, o_ref, m_sc, l_sc, acc_sc, *, tq, tk, rb):
    step = pl.program_id(1)
    qi = qi_ref[step]
    ki = ki_ref[step]
    rows = tq * H_B

    @pl.when(ki == 0)
    def _():
        m_sc[...] = jnp.full(m_sc.shape, -jnp.inf, F32)
        l_sc[...] = jnp.zeros(l_sc.shape, F32)
        acc_sc[...] = jnp.zeros(acc_sc.shape, F32)

    def update(masked):
        k = k_ref[...]
        v = k[:, :KV_LORA]
        for r0 in range(0, rows, rb):
            rs = slice(r0, r0 + rb)
            s = _dot(q_ref[rs, :], k, NT)
            if masked:
                tok = qi * tq + (r0 + lax.broadcasted_iota(jnp.int32, (rb, tk), 0)) // H_B
                col = ki * tk + lax.broadcasted_iota(jnp.int32, (rb, tk), 1)
                s = jnp.where(col <= tok, s, -jnp.inf)
            m_prev = m_sc[rs, :]
            m_new = jnp.maximum(m_prev, jnp.max(s, axis=-1, keepdims=True))
            corr = jnp.exp2(m_prev - m_new)
            p = jnp.exp2(s - m_new)
            l_sc[rs, :] = l_sc[rs, :] * corr + jnp.sum(p, axis=-1, keepdims=True)
            acc_sc[rs, :] = acc_sc[rs, :] * corr + _dot(p.astype(BF16), v)
            m_sc[rs, :] = m_new

    crosses_diagonal = (ki + 1) * tk - 1 > qi * tq

    @pl.when(crosses_diagonal)
    def _():
        update(True)

    @pl.when(jnp.logical_not(crosses_diagonal))
    def _():
        update(False)

    @pl.when((ki + 1) * tk >= (qi + 1) * tq)
    def _():
        o_ref[...] = (acc_sc[...] / l_sc[...]).astype(o_ref.dtype)


def mla_prompt_attention(qcat, kcat, n_seq, seq):
    tq, tk = min(FLASH_TQ, seq), min(FLASH_TK, seq)
    assert seq % tq == 0 and seq % tk == 0
    nq, nkb = seq // tq, seq // tk
    rows = tq * H_B
    rb = min(FLASH_RB, rows)
    qi_list, ki_list = [], []
    for qi in range(nq):
        for ki in range(-(-((qi + 1) * tq) // tk)):
            qi_list.append(qi)
            ki_list.append(ki)
    qi_tbl = jnp.asarray(np.array(qi_list, np.int32))
    ki_tbl = jnp.asarray(np.array(ki_list, np.int32))
    grid_spec = pltpu.PrefetchScalarGridSpec(
        num_scalar_prefetch=2,
        grid=(n_seq, len(qi_list)),
        in_specs=[pl.BlockSpec((rows, QCAT), lambda b, s, qt, kt: (b * nq + qt[s], 0)),
                  pl.BlockSpec((tk, QCAT), lambda b, s, qt, kt: (b * nkb + kt[s], 0))],
        out_specs=pl.BlockSpec((rows, KV_LORA), lambda b, s, qt, kt: (b * nq + qt[s], 0)),
        scratch_shapes=[pltpu.VMEM((rows, 1), F32), pltpu.VMEM((rows, 1), F32), pltpu.VMEM((rows, KV_LORA), F32)],
    )
    o = pl.pallas_call(
        functools.partial(_flash_kernel, tq=tq, tk=tk, rb=rb),
        grid_spec=grid_spec,
        out_shape=jax.ShapeDtypeStruct((n_seq * seq * H_B, KV_LORA), BF16),
        compiler_params=_params(("arbitrary", "arbitrary")),
        name="mla_prompt_attn",
    )(qi_tbl, ki_tbl, qcat.reshape(n_seq * seq * H_B, QCAT), kcat)
    return o.reshape(n_seq * seq, H_B * KV_LORA)


PAGES_PER_CHUNK = 16


def _paged_kernel(pt_ref, q_ref, knew_ref, ckv_hbm, ckr_hbm, o_ref, kv_buf, kr_buf, sems, *,
                  n_new, layer, page, chunk_pages, n_chunks):
    b = pl.program_id(0)
    n_b = pl.num_programs(0)
    rows = n_new * H_B

    def page_copies(seq, chunk, slot, j):
        phys = pt_ref[seq, chunk * chunk_pages + j]
        dst = pl.ds(pl.multiple_of(j * page, page), page)
        return (pltpu.make_async_copy(ckv_hbm.at[layer, phys], kv_buf.at[slot, dst], sems.at[0, slot]),
                pltpu.make_async_copy(ckr_hbm.at[layer, phys], kr_buf.at[slot, dst], sems.at[1, slot]))

    def start_chunk(seq, chunk, slot):
        def body(j, carry):
            for cp in page_copies(seq, chunk, slot, j):
                cp.start()
            return carry
        lax.fori_loop(0, chunk_pages, body, 0)

    def wait_chunk(seq, chunk, slot):
        def body(j, carry):
            for cp in page_copies(seq, chunk, slot, j):
                cp.wait()
            return carry
        lax.fori_loop(0, chunk_pages, body, 0)

    @pl.when(b == 0)
    def _():
        start_chunk(0, 0, 0)

    q = q_ref[0]
    qf = q.astype(F32)
    kn = knew_ref[0].astype(F32)
    tok = lax.broadcasted_iota(jnp.int32, (rows, 1), 0) // H_B
    cols = []
    for j in range(n_new):
        sj = jnp.sum(qf * kn[j:j + 1, :], axis=-1, keepdims=True)
        cols.append(jnp.where(j <= tok, sj, -jnp.inf))
    m = cols[0]
    for j in range(1, n_new):
        m = jnp.maximum(m, cols[j])
    l = jnp.zeros((rows, 1), F32)
    acc = jnp.zeros((rows, KV_LORA), F32)
    for j in range(n_new):
        pj = jnp.exp2(cols[j] - m)
        l = l + pj
        acc = acc + pj * kn[j:j + 1, :KV_LORA]

    q_lat = q[:, :KV_LORA]
    q_rope = q[:, KV_LORA:KV_LORA + QK_ROPE]
    for c in range(n_chunks):
        slot = c % 2
        if c + 1 < n_chunks:
            start_chunk(b, c + 1, 1 - slot)
        else:
            @pl.when(b + 1 < n_b)
            def _():
                start_chunk(b + 1, 0, 1 - slot)
        wait_chunk(b, c, slot)
        kv = kv_buf[slot].astype(BF16)
        kr = kr_buf[slot].astype(BF16)
        s = _dot(q_lat, kv, NT) + _dot(q_rope, kr, NT)
        m_new = jnp.maximum(m, jnp.max(s, axis=-1, keepdims=True))
        corr = jnp.exp2(m - m_new)
        p = jnp.exp2(s - m_new)
        l = l * corr + jnp.sum(p, axis=-1, keepdims=True)
        acc = acc * corr + _dot(p.astype(BF16), kv)
        m = m_new
    o_ref[0] = (acc / l).astype(o_ref.dtype)


def mla_sample_attention(qcat, kcat, cache_kv, cache_kr, page_table, layer, n_seq, n_new):
    n_pages = page_table.shape[1]
    page = cache_kv.shape[2]
    rows = n_new * H_B
    chunk_pages = min(PAGES_PER_CHUNK, n_pages // 2)
    assert n_pages % (2 * chunk_pages) == 0
    n_chunks = n_pages // chunk_pages
    q3 = qcat.reshape(n_seq, rows, QCAT)
    k3 = kcat.reshape(n_seq, n_new, QCAT)
    grid_spec = pltpu.PrefetchScalarGridSpec(
        num_scalar_prefetch=1,
        grid=(n_seq,),
        in_specs=[pl.BlockSpec((1, rows, QCAT), lambda b, pt: (b, 0, 0)),
                  pl.BlockSpec((1, n_new, QCAT), lambda b, pt: (b, 0, 0)),
                  pl.BlockSpec(memory_space=pl.ANY),
                  pl.BlockSpec(memory_space=pl.ANY)],
        out_specs=pl.BlockSpec((1, rows, KV_LORA), lambda b, pt: (b, 0, 0)),
        scratch_shapes=[pltpu.VMEM((2, chunk_pages * page, KV_LORA), F32),
                        pltpu.VMEM((2, chunk_pages * page, QK_ROPE), F32),
                        pltpu.SemaphoreType.DMA((2, 2))],
    )
    o = pl.pallas_call(
        functools.partial(_paged_kernel, n_new=n_new, layer=layer, page=page, chunk_pages=chunk_pages,
                          n_chunks=n_chunks),
        grid_spec=grid_spec,
        out_shape=jax.ShapeDtypeStruct((n_seq, rows, KV_LORA), BF16),
        compiler_params=_params(("arbitrary",)),
        name="mla_sample_attn",
    )(page_table, q3, k3, cache_kv, cache_kr)
    return o.reshape(n_seq * n_new, H_B * KV_LORA)


def _gla_cumsum_matrix(chunk):
    blocks = [np.tril(np.ones((chunk, chunk), np.float32))]
    m = chunk // 2
    while m >= 1:
        w = np.zeros((chunk, chunk), np.float32)
        for t in range(chunk):
            seg = (t // m) * m
            if (t % (2 * m)) >= m:
                w[t, seg:t + 1] = 1.0
            else:
                w[t, t + 1:seg + m] = 1.0
        blocks.append(w)
        m //= 2
    return np.concatenate(blocks, axis=0)


def _gla_kernel(q_ref, k_ref, v_ref, g_ref, wc_ref, s0_ref, o_ref, sfin_ref, s_sc, *, chunk):
    c_idx = pl.program_id(1)

    @pl.when(c_idx == 0)
    def _():
        s_sc[...] = s0_ref[0]

    n_lev = chunk.bit_length() - 1
    row = lax.broadcasted_iota(jnp.int32, (chunk, chunk), 0)
    col = lax.broadcasted_iota(jnp.int32, (chunk, chunk), 1)
    row_in_chunk = lax.broadcasted_iota(jnp.int32, (chunk, DK_A), 0)
    wc = wc_ref[...]
    for hh in range(H_A):
        q = q_ref[:, hh * DK_A:(hh + 1) * DK_A]
        k = k_ref[:, hh * DK_A:(hh + 1) * DK_A]
        v = v_ref[:, hh * DV_A:(hh + 1) * DV_A].astype(BF16)
        g = g_ref[:, hh * DK_A:(hh + 1) * DK_A]
        sums = _dot_hi_exact_lhs(wc, g)
        b = sums[:chunk]
        b_last = b[chunk - 1:chunk, :]
        s_prev = s_sc[hh]

        a = jnp.where(row == col, _dot(q.astype(BF16), k.astype(BF16), NT), 0.0)
        for lev in range(n_lev):
            m = chunk >> (lev + 1)
            scale = jnp.exp(sums[(lev + 1) * chunk:(lev + 2) * chunk])
            upper = (row_in_chunk % (2 * m)) >= m
            q_l = jnp.where(upper, q * scale, 0.0).astype(BF16)
            k_l = jnp.where(upper, 0.0, k * scale).astype(BF16)
            same_block = (row // (2 * m)) == (col // (2 * m))
            a = a + jnp.where(same_block, _dot(q_l, k_l, NT), 0.0)

        inter = _dot((q * jnp.exp(b)).astype(BF16), s_prev.astype(BF16))
        o_ref[:, hh * DV_A:(hh + 1) * DV_A] = inter + _dot(a.astype(BF16), v)

        k_dec = (k * jnp.exp(b_last - b)).astype(BF16)
        decay_col = jnp.transpose(jnp.broadcast_to(jnp.exp(b_last), (DK_A, DK_A)))[:, :1]
        s_sc[hh] = decay_col * s_prev + _dot(k_dec, v, TN)

    @pl.when(c_idx == pl.num_programs(1) - 1)
    def _():
        sfin_ref[0] = s_sc[...]


def gla(qg, kg, vg, logf, s0, n_seq, seq, chunk):
    assert seq % chunk == 0 and chunk & (chunk - 1) == 0 and chunk % 8 == 0
    n_chunks = seq // chunk
    wc = jnp.asarray(_gla_cumsum_matrix(chunk), BF16)
    nk = H_A * DK_A

    def tok(cols):
        return pl.BlockSpec((chunk, cols), lambda b, c: (b * n_chunks + c, 0))

    state_spec = pl.BlockSpec((1, H_A, DK_A, DV_A), lambda b, c: (b, 0, 0, 0))
    return pl.pallas_call(
        functools.partial(_gla_kernel, chunk=chunk),
        grid=(n_seq, n_chunks),
        in_specs=[tok(nk), tok(nk), tok(H_A * DV_A), tok(nk), _const_spec(wc.shape), state_spec],
        out_specs=[tok(H_A * DV_A), state_spec],
        out_shape=[jax.ShapeDtypeStruct((n_seq * seq, H_A * DV_A), F32),
                   jax.ShapeDtypeStruct((n_seq, H_A, DK_A, DV_A), F32)],
        scratch_shapes=[pltpu.VMEM((H_A, DK_A, DV_A), F32)],
        compiler_params=_params(("arbitrary", "arbitrary")),
        name="gla",
    )(qg, kg, vg, logf, wc, s0)


def _route(h2, wrt_ref, rbias_ref, tm):
    logits = _dot_hi(wrt_ref[...], h2, NT)
    scores = _sigmoid(logits)
    sel = scores + rbias_ref[...]
    neg = -jnp.inf

    iota_g = lax.broadcasted_iota(jnp.int32, (GROUP_SIZE, tm), 0).astype(F32)
    g_rows = []
    for g in range(N_GROUPS):
        blk = sel[g * GROUP_SIZE:(g + 1) * GROUP_SIZE]
        m1 = jnp.max(blk, axis=0, keepdims=True)
        first = jnp.min(jnp.where(blk == m1, iota_g, float(GROUP_SIZE)), axis=0, keepdims=True)
        m2 = jnp.max(jnp.where(iota_g == first, neg, blk), axis=0, keepdims=True)
        g_rows.append(m1 + m2)
    g_score = jnp.concatenate(g_rows, axis=0)

    iota_n = lax.broadcasted_iota(jnp.int32, (N_GROUPS, tm), 0).astype(F32)
    g_keep = jnp.zeros((N_GROUPS, tm), F32)
    for _ in range(TOPK_GROUPS):
        mx = jnp.max(g_score, axis=0, keepdims=True)
        first = jnp.min(jnp.where(g_score == mx, iota_n, float(N_GROUPS)), axis=0, keepdims=True)
        hit = iota_n == first
        g_keep = jnp.where(hit, 1.0, g_keep)
        g_score = jnp.where(hit, neg, g_score)

    sel_m = jnp.concatenate(
        [jnp.where(g_keep[g:g + 1] > 0.0, sel[g * GROUP_SIZE:(g + 1) * GROUP_SIZE], neg) for g in range(N_GROUPS)], axis=0)

    iota_e = lax.broadcasted_iota(jnp.int32, (N_EXPERTS, tm), 0).astype(F32)
    ids, wts, hits = [], [], []
    for _ in range(TOP_K):
        mx = jnp.max(sel_m, axis=0, keepdims=True)
        first = jnp.min(jnp.where(sel_m == mx, iota_e, float(N_EXPERTS)), axis=0, keepdims=True)
        hit = iota_e == first
        ids.append(first)
        hits.append(hit)
        wts.append(jnp.sum(jnp.where(hit, scores, 0.0), axis=0, keepdims=True))
        sel_m = jnp.where(hit, neg, sel_m)
    ids = jnp.concatenate(ids, axis=0)
    wts = jnp.concatenate(wts, axis=0)
    wts = wts / jnp.sum(wts, axis=0, keepdims=True) * ROUTED_SCALE
    return ids, wts, hits


def _rank_in_expert(hits, cnt_sc, tm):
    chosen = jnp.where(hits[0], 1.0, 0.0)
    for hit in hits[1:]:
        chosen = jnp.where(hit, 1.0, chosen)
    earlier = (lax.broadcasted_iota(jnp.int32, (tm, tm), 0) < lax.broadcasted_iota(jnp.int32, (tm, tm), 1))
    prefix = _dot(chosen.astype(BF16), jnp.where(earlier, 1.0, 0.0).astype(BF16))
    base = cnt_sc[:, :1] + prefix
    ranks = [jnp.sum(jnp.where(hit, base, 0.0), axis=0, keepdims=True) for hit in hits]
    cnt_sc[...] = cnt_sc[...] + jnp.sum(chosen, axis=1, keepdims=True)
    return jnp.concatenate(ranks, axis=0)


def _merge_kernel(x_ref, scm_ref, shm_ref, gm_ref, scf_ref, shf_ref, olat_ref, ogla_ref,
                  w3_ref, wuv_ref, wb_ref, wa_ref, wo_ref, ggla_ref, ln1g_ref, ln1b_ref, wrt_ref, rbias_ref, cnt0_ref,
                  x1_ref, h2_ref, ids_ref, wts_ref, rank_ref, cnt_ref, cnt_sc, *, alpha):
    @pl.when(pl.program_id(0) == 0)
    def _():
        cnt_sc[...] = cnt0_ref[...]

    gb, rb, d = x_ref.shape
    tm = gb * rb
    x = x_ref[...]
    h = (x * (1.0 + scm_ref[...]) + shm_ref[...]).reshape(tm, d).astype(BF16)
    z3 = _dot(h, w3_ref[...])

    vb = [_dot(olat_ref[:, hh * KV_LORA:(hh + 1) * KV_LORA], wuv_ref[hh]).astype(BF16) for hh in range(H_B)]
    y_b = _dot(jnp.concatenate(vb, axis=-1), wb_ref[...])

    ga = []
    for hh in range(H_A):
        o_n = _rms(ogla_ref[:, hh * DV_A:(hh + 1) * DV_A], ggla_ref[...])
        ga.append((o_n * _silu(z3[:, hh * DV_A:(hh + 1) * DV_A])).astype(BF16))
    y_a = _dot(jnp.concatenate(ga, axis=-1), wa_ref[...])

    merged = _sigmoid(z3[:, d:2 * d]) * y_a + _sigmoid(z3[:, 2 * d:]) * y_b
    mix = _dot(merged.astype(BF16), wo_ref[...]).reshape(gb, rb, d)
    x1 = _layer_norm(alpha * x + gm_ref[...] * mix, ln1g_ref[...], ln1b_ref[...])
    x1_ref[...] = x1
    h2 = (x1 * (1.0 + scf_ref[...]) + shf_ref[...]).reshape(tm, d)
    h2_ref[...] = h2.astype(BF16)
    ids, wts, hits = _route(h2, wrt_ref, rbias_ref, tm)
    ids_ref[...] = ids.astype(jnp.int32)
    wts_ref[...] = wts
    rank_ref[...] = _rank_in_expert(hits, cnt_sc, tm).astype(jnp.int32)
    cnt_ref[...] = cnt_sc[...]


def merge_and_route(x, mods, olat, ogla, mw, cnt0, alpha):
    g, r, d = x.shape
    t = g * r
    x_spec, mod_spec, n_tiles = _group_specs(g, r, d)
    names = ["w3", "wuv", "wb", "wa", "wo", "ggla", "ln1g", "ln1b", "wrt", "rbias"]
    kt_spec = pl.BlockSpec((TOP_K, TOKEN_TILE), lambda i: (0, i))
    cnt_spec = _const_spec((N_EXPERTS, LANES))
    return pl.pallas_call(
        functools.partial(_merge_kernel, alpha=alpha),
        grid=(n_tiles,),
        in_specs=[x_spec] + [mod_spec] * 5 + [_tok_spec(H_B * KV_LORA), _tok_spec(H_A * DV_A)]
                 + [_const_spec(mw[n].shape) for n in names] + [cnt_spec],
        out_specs=[x_spec, _tok_spec(d), kt_spec, kt_spec, kt_spec, cnt_spec],
        out_shape=[jax.ShapeDtypeStruct((g, r, d), F32), jax.ShapeDtypeStruct((t, d), BF16),
                   jax.ShapeDtypeStruct((TOP_K, t), jnp.int32), jax.ShapeDtypeStruct((TOP_K, t), F32),
                   jax.ShapeDtypeStruct((TOP_K, t), jnp.int32), jax.ShapeDtypeStruct((N_EXPERTS, LANES), F32)],
        scratch_shapes=[pltpu.VMEM((N_EXPERTS, LANES), F32)],
        compiler_params=_params(("arbitrary",)),
        name="merge_route",
    )(x, *mods, olat, ogla, *[mw[n] for n in names], cnt0)


def _moe_kernel(blk_e_ref, n_used_ref, x_ref, wg_ref, wu_ref, wd_ref, y_ref, wg_sc, wu_sc, wd_sc):
    i = pl.program_id(0)
    new_expert = jnp.logical_or(i == 0, blk_e_ref[i] != blk_e_ref[jnp.maximum(i - 1, 0)])

    @pl.when(new_expert)
    def _():
        wg_sc[...] = wg_ref[0].astype(BF16)
        wu_sc[...] = wu_ref[0].astype(BF16)
        wd_sc[...] = wd_ref[0].astype(BF16)

    @pl.when(i < n_used_ref[0])
    def _():
        x = x_ref[...]
        gate = _dot(x, wg_sc[...])
        up = _dot(x, wu_sc[...])
        act = (_silu(gate) * up).astype(BF16)
        y_ref[...] = _dot(act, wd_sc[...]).astype(y_ref.dtype)

    @pl.when(i >= n_used_ref[0])
    def _():
        y_ref[...] = jnp.zeros(y_ref.shape, y_ref.dtype)


def routed_expert_blocks(x_sorted, blk_e, n_used, w_gate, w_up, w_down):
    m_pad, d = x_sorted.shape
    n_blk = m_pad // MOE_ROWS
    de = w_gate.shape[-1]
    grid_spec = pltpu.PrefetchScalarGridSpec(
        num_scalar_prefetch=2,
        grid=(n_blk,),
        in_specs=[pl.BlockSpec((MOE_ROWS, d), lambda i, be, nu: (i, 0)),
                  pl.BlockSpec((1, d, de), lambda i, be, nu: (be[i], 0, 0)),
                  pl.BlockSpec((1, d, de), lambda i, be, nu: (be[i], 0, 0)),
                  pl.BlockSpec((1, de, d), lambda i, be, nu: (be[i], 0, 0))],
        out_specs=pl.BlockSpec((MOE_ROWS, d), lambda i, be, nu: (i, 0)),
        scratch_shapes=[pltpu.VMEM((d, de), BF16), pltpu.VMEM((d, de), BF16), pltpu.VMEM((de, d), BF16)],
    )
    return pl.pallas_call(
        _moe_kernel,
        grid_spec=grid_spec,
        out_shape=jax.ShapeDtypeStruct((m_pad, d), BF16),
        compiler_params=_params(("arbitrary",)),
        name="moe_experts",
    )(blk_e, n_used, x_sorted, w_gate, w_up, w_down)


def _pos_kernel(ids_ref, rank_ref, pstart_ref, pos_ref):
    k, tm = ids_ref.shape
    iota_e = lax.broadcasted_iota(jnp.int32, (N_EXPERTS, tm), 0)
    pstart = pstart_ref[:, :1]
    base = [jnp.sum(jnp.where(iota_e == ids_ref[j:j + 1, :], pstart, 0.0), axis=0, keepdims=True) for j in range(k)]
    pos_ref[...] = jnp.concatenate(base, axis=0).astype(jnp.int32) + rank_ref[...]


def dispatch(ids, rank, counts):
    k, t = ids.shape
    m = k * t
    counts = counts[:, 0].astype(jnp.int32)
    padded = (counts + MOE_ROWS - 1) // MOE_ROWS * MOE_ROWS
    pend = jnp.cumsum(padded)
    pstart = jnp.broadcast_to((pend - padded).astype(F32)[:, None], (N_EXPERTS, LANES))
    kt_spec = pl.BlockSpec((k, TOKEN_TILE), lambda i: (0, i))
    pos = pl.pallas_call(
        _pos_kernel,
        grid=(t // TOKEN_TILE,),
        in_specs=[kt_spec, kt_spec, _const_spec((N_EXPERTS, LANES))],
        out_specs=kt_spec,
        out_shape=jax.ShapeDtypeStruct((k, t), jnp.int32),
        compiler_params=_params(("arbitrary",)),
        name="dispatch_positions",
    )(ids, rank, pstart)
    m_pad = -(-(m + N_EXPERTS * (MOE_ROWS - 1)) // MOE_ROWS) * MOE_ROWS
    n_blk = m_pad // MOE_ROWS
    tok = jnp.broadcast_to(jnp.arange(t, dtype=jnp.int32)[None, :], (k, t))
    tok_pad = jnp.zeros((m_pad,), jnp.int32).at[pos.reshape(m)].set(tok.reshape(m), unique_indices=True)
    blk_e = jnp.minimum(jnp.searchsorted(pend, jnp.arange(n_blk) * MOE_ROWS, side="right"), N_EXPERTS - 1).astype(jnp.int32)
    n_used = (pend[-1] // MOE_ROWS).astype(jnp.int32).reshape(1)
    return tok_pad, pos, blk_e, n_used


def _final_kernel(x1_ref, gf_ref, h2_ref, yg_ref, wts_ref, wsg_ref, wsu_ref, wsd_ref, ln2g_ref, ln2b_ref, y_ref, *, alpha):
    gb, rb, d = x1_ref.shape
    h2 = h2_ref[...]
    act = (_silu(_dot(h2, wsg_ref[...])) * _dot(h2, wsu_ref[...])).astype(BF16)
    ffn = _dot(act, wsd_ref[...])
    w = wts_ref[...]
    routed = yg_ref[0].astype(F32) * w[:, 0:1]
    for j in range(1, TOP_K):
        routed = routed + yg_ref[j].astype(F32) * w[:, j:j + 1]
    ffn = (ffn + routed).reshape(gb, rb, d)
    y_ref[...] = _layer_norm(alpha * x1_ref[...] + gf_ref[...] * ffn, ln2g_ref[...], ln2b_ref[...])


def shared_combine_norm(x1, gf, h2, yg, wts_t, fw, alpha):
    g, r, d = x1.shape
    x_spec, mod_spec, n_tiles = _group_specs(g, r, d)
    names = ["wsg", "wsu", "wsd", "ln2g", "ln2b"]
    return pl.pallas_call(
        functools.partial(_final_kernel, alpha=alpha),
        grid=(n_tiles,),
        in_specs=[x_spec, mod_spec, _tok_spec(d), pl.BlockSpec((TOP_K, TOKEN_TILE, d), lambda i: (0, i, 0)),
                  _tok_spec(TOP_K)] + [_const_spec(fw[n].shape) for n in names],
        out_specs=x_spec,
        out_shape=jax.ShapeDtypeStruct((g, r, d), F32),
        compiler_params=_params(("arbitrary",)),
        name="shared_combine_norm",
    )(x1, gf, h2, yg, wts_t, *[fw[n] for n in names])


def _rot_cols(w):
    half = w.shape[-1] // 2
    return jnp.concatenate([-w[..., half:], w[..., :half]], axis=-1)


def _pad_cols(w, width):
    return jnp.pad(w, [(0, 0)] * (w.ndim - 1) + [(0, width - w.shape[-1])])


def _prepare_weights(w):
    cuts = np.cumsum((0,) + IN_SIZES)
    part = [w["w_in"][:, cuts[i]:cuts[i + 1]] for i in range(len(IN_SIZES))]
    w_qa, w_kva, w_kr, w_gq, w_gk, w_gv, w_go, w_gf, w_ga, w_gb = part
    w1 = jnp.concatenate([w_qa, w_kva, _pad_cols(w_kr, LANES), _pad_cols(_rot_cols(w_kr), LANES)], axis=1)
    wuq = w["w_uq"].reshape(Q_LORA, H_B, QK_NOPE + QK_ROPE)
    wuq_rope = wuq[:, :, QK_NOPE:]
    wq = jnp.concatenate([wuq[:, :, :QK_NOPE].reshape(Q_LORA, H_B * QK_NOPE),
                          _pad_cols(wuq_rope, LANES).reshape(Q_LORA, H_B * LANES),
                          _pad_cols(_rot_cols(wuq_rope), LANES).reshape(Q_LORA, H_B * LANES)], axis=1)
    wg = jnp.concatenate([w_gq, w_gk, w_gv, _pad_cols(w_gf, LANES)], axis=1)
    proj = dict(
        w1=w1.astype(BF16), wq=wq.astype(BF16), wg=wg.astype(BF16),
        wuk=jnp.transpose(w["w_uk"], (1, 2, 0)).astype(BF16),
        wf2=jnp.pad(w["w_gla_f2"], ((0, LANES - GLA_LR), (0, 0))).astype(BF16),
        bf=w["b_gla_f"].reshape(1, -1), gq=w["g_q_norm"].reshape(1, -1), gkv=w["g_kv_norm"].reshape(1, -1))
    merge = dict(
        w3=jnp.concatenate([w_go, w_ga, w_gb], axis=1).astype(BF16),
        wuv=jnp.transpose(w["w_uv"], (1, 0, 2)).astype(BF16),
        wb=w["w_b_out"].astype(BF16), wa=w["w_a_out"].astype(BF16), wo=w["w_o"].astype(BF16),
        ggla=w["g_gla_norm"].reshape(1, -1), ln1g=w["ln1_g"].reshape(1, -1), ln1b=w["ln1_b"].reshape(1, -1),
        wrt=jnp.transpose(w["w_router"]), rbias=w["router_bias"].reshape(-1, 1))
    final = dict(
        wsg=w["w_s_gate"].astype(BF16), wsu=w["w_s_up"].astype(BF16), wsd=w["w_s_down"].astype(BF16),
        ln2g=w["ln2_g"].reshape(1, -1), ln2b=w["ln2_b"].reshape(1, -1))
    return proj, merge, final


def _rope_tables(pos, tile_rows):
    half = QK_ROPE // 2
    freqs = ROPE_THETA ** (-jnp.arange(half, dtype=F32) / half)
    ang = pos.astype(F32)[:, None] * freqs
    cos = _pad_cols(jnp.concatenate([jnp.cos(ang)] * 2, axis=-1), LANES)
    sin = _pad_cols(jnp.concatenate([jnp.sin(ang)] * 2, axis=-1), LANES)
    if pos.shape[0] < tile_rows:
        rep = tile_rows // pos.shape[0]
        cos, sin = jnp.tile(cos, (rep, 1)), jnp.tile(sin, (rep, 1))
    return cos, sin


def _split_mods(mod, n):
    return [m.reshape(n, 1, D_MODEL) for m in jnp.split(mod, 6, axis=-1)]


def kernel(x_prompt, x_sample, c_prompt, c_sample, cache_kv_latent, cache_k_rope, state_gla, page_table, w_ada, b_ada, w_in, g_q_norm, w_uq, g_kv_norm, w_uk, w_uv, w_gla_f2, b_gla_f, g_gla_norm, w_a_out, w_b_out, w_o, ln1_g, ln1_b, w_router, router_bias, w_e_gate, w_e_up, w_e_down, w_s_gate, w_s_up, w_s_down, ln2_g, ln2_b):
    w_all = dict(w_ada=w_ada, b_ada=b_ada, w_in=w_in, g_q_norm=g_q_norm, w_uq=w_uq, g_kv_norm=g_kv_norm, w_uk=w_uk,
                 w_uv=w_uv, w_gla_f2=w_gla_f2, b_gla_f=b_gla_f, g_gla_norm=g_gla_norm, w_a_out=w_a_out,
                 w_b_out=w_b_out, w_o=w_o, ln1_g=ln1_g, ln1_b=ln1_b, w_router=w_router, router_bias=router_bias,
                 w_e_gate=w_e_gate, w_e_up=w_e_up, w_e_down=w_e_down, w_s_gate=w_s_gate, w_s_up=w_s_up,
                 w_s_down=w_s_down, ln2_g=ln2_g, ln2_b=ln2_b)
    depth = w_ada.shape[0]
    alpha = (2.0 * depth) ** 0.25
    n_p, s_p, _ = x_prompt.shape
    n_s, s_s, _ = x_sample.shape
    t_p, t_s = n_p * s_p, n_s * s_s
    past_len = page_table.shape[1] * cache_kv_latent.shape[2]
    cos_p, sin_p = _rope_tables(jnp.arange(s_p, dtype=jnp.int32), TOKEN_TILE)
    cos_s, sin_s = _rope_tables(past_len + jnp.arange(s_s, dtype=jnp.int32), TOKEN_TILE)

    n_c = n_p + n_s
    n_c_pad = -(-n_c // 16) * 16
    c_all = jnp.pad(jnp.concatenate([c_prompt, c_sample], axis=0), ((0, n_c_pad - n_c), (0, 0)))

    yp, ys = x_prompt, x_sample
    outs = [[] for _ in range(6)]
    for layer in range(depth):
        w = {name: arr[layer] for name, arr in w_all.items()}
        pw, mw, fw = _prepare_weights(w)
        mod = adaln(c_all, w["w_ada"], w["b_ada"])
        mods_p = _split_mods(mod[:n_p], n_p)
        mods_s = _split_mods(mod[n_p:n_c], n_s)

        ckv_p, kr_p, kcat_p, qcat_p, qg_p, kg_p, vg_p, lf_p = input_projections(yp, mods_p[1], mods_p[0], cos_p, sin_p, pw)
        olat_p = mla_prompt_attention(qcat_p, kcat_p, n_p, s_p)
        s0_p = jnp.zeros((n_p, H_A, DK_A, DV_A), F32)
        ogla_p, sfin_p = gla(qg_p, kg_p, vg_p, lf_p, s0_p, n_p, s_p, min(GLA_CHUNK_PROMPT, s_p))
        cnt0 = jnp.zeros((N_EXPERTS, LANES), F32)
        x1_p, h2_p, ids_p, wts_p, rank_p, cnt_p = merge_and_route(
            yp, [mods_p[1], mods_p[0], mods_p[2], mods_p[4], mods_p[3]], olat_p, ogla_p, mw, cnt0, alpha)

        ckv_s, kr_s, kcat_s, qcat_s, qg_s, kg_s, vg_s, lf_s = input_projections(ys, mods_s[1], mods_s[0], cos_s, sin_s, pw)
        olat_s = mla_sample_attention(qcat_s, kcat_s, cache_kv_latent, cache_k_rope, page_table, layer, n_s, s_s)
        ogla_s, sfin_s = gla(qg_s, kg_s, vg_s, lf_s, state_gla[layer], n_s, s_s, s_s)
        x1_s, h2_s, ids_s, wts_s, rank_s, cnt = merge_and_route(
            ys, [mods_s[1], mods_s[0], mods_s[2], mods_s[4], mods_s[3]], olat_s, ogla_s, mw, cnt_p, alpha)

        h2 = jnp.concatenate([h2_p, h2_s], axis=0)
        ids = jnp.concatenate([ids_p, ids_s], axis=1)
        rank = jnp.concatenate([rank_p, rank_s], axis=1)
        tok_pad, pos, blk_e, n_used = dispatch(ids, rank, cnt)
        y_sorted = routed_expert_blocks(jnp.take(h2, tok_pad, axis=0), blk_e, n_used, w["w_e_gate"], w["w_e_up"], w["w_e_down"])
        yg_p = jnp.take(y_sorted, pos[:, :t_p], axis=0)
        yg_s = jnp.take(y_sorted, pos[:, t_p:], axis=0)

        yp = shared_combine_norm(x1_p, mods_p[5], h2_p, yg_p, jnp.transpose(wts_p), fw, alpha)
        ys = shared_combine_norm(x1_s, mods_s[5], h2_s, yg_s, jnp.transpose(wts_s), fw, alpha)

        for lst, val in zip(outs, [ckv_p.reshape(n_p, s_p, KV_LORA), kr_p.reshape(n_p, s_p, QK_ROPE), sfin_p,
                                   ckv_s.reshape(n_s, s_s, KV_LORA), kr_s.reshape(n_s, s_s, QK_ROPE), sfin_s]):
            lst.append(val.astype(state_gla.dtype) if val.ndim == 4 else val)
    return (yp, ys) + tuple(jnp.stack(o) for o in outs)
```

```python
import functools

import numpy as np
import jax
import jax.numpy as jnp
from jax import lax
from jax.experimental import pallas as pl
from jax.experimental.pallas import tpu as pltpu

F32 = jnp.float32
BF16 = jnp.bfloat16

D_MODEL = 1024
H_A, DK_A, DV_A, GLA_LR, GLA_TAU = 4, 128, 256, 16, 16.0
H_B, Q_LORA, KV_LORA, QK_NOPE, QK_ROPE, V_HEAD = 8, 384, 256, 128, 64, 128
ROPE_THETA = 10000.0
ATTN_SCALE = (QK_NOPE + QK_ROPE) ** -0.5
Q_SCALE = ATTN_SCALE * 1.4426950408889634
N_EXPERTS, TOP_K, N_GROUPS, TOPK_GROUPS = 256, 8, 8, 4
GROUP_SIZE = N_EXPERTS // N_GROUPS
D_EXPERT, D_SHARED, ROUTED_SCALE = 256, 256, 2.5
EPS = 1e-6
IN_SIZES = (Q_LORA, KV_LORA, QK_ROPE, H_A * DK_A, H_A * DK_A, H_A * DV_A, H_A * DV_A, GLA_LR, D_MODEL, D_MODEL)

LANES = 128
QCAT = KV_LORA + LANES
VMEM_LIMIT = 56 * 1024 * 1024

TOKEN_TILE = 256
FLASH_TQ, FLASH_TK = 256, 512
FLASH_HEADS_PER_BLOCK = 8
assert FLASH_TQ == TOKEN_TILE
GLA_CHUNK_PROMPT = 128
MOE_ROWS = 256

NN = (((1,), (0,)), ((), ()))
NT = (((1,), (1,)), ((), ()))
TN = (((0,), (0,)), ((), ()))


def _dot(a, b, dims=NN):
    return lax.dot_general(a, b, dims, preferred_element_type=F32)


def _split3(x):
    x1 = x.astype(BF16)
    r1 = x - x1.astype(F32)
    x2 = r1.astype(BF16)
    x3 = (r1 - x2.astype(F32)).astype(BF16)
    return x1, x2, x3


def _dot_hi(a, b, dims=NN):
    a1, a2, a3 = _split3(a)
    b1, b2, b3 = _split3(b)
    small = _dot(a3, b1, dims) + _dot(a2, b2, dims) + _dot(a1, b3, dims)
    mid = _dot(a2, b1, dims) + _dot(a1, b2, dims)
    return (small + mid) + _dot(a1, b1, dims)


def _dot_hi_exact_lhs(w, x, dims=NN):
    x1, x2, x3 = _split3(x)
    return (_dot(w, x3, dims) + _dot(w, x2, dims)) + _dot(w, x1, dims)


def _sigmoid(x):
    return 1.0 / (1.0 + jnp.exp(-x))


def _silu(x):
    return x * _sigmoid(x)


def _rms(x, g):
    return x * lax.rsqrt(jnp.mean(x * x, axis=-1, keepdims=True) + EPS) * g


def _layer_norm(x, g, b):
    mu = jnp.mean(x, axis=-1, keepdims=True)
    xc = x - mu
    var = jnp.mean(xc * xc, axis=-1, keepdims=True)
    return xc * lax.rsqrt(var + EPS) * g + b


def _params(sem):
    return pltpu.CompilerParams(dimension_semantics=sem, vmem_limit_bytes=VMEM_LIMIT)


def _const_spec(shape):
    nd = len(shape)
    return pl.BlockSpec(shape, lambda *_: (0,) * nd)


def _adaln_kernel(c_ref, w_ref, b_ref, o_ref):
    o_ref[...] = _dot_hi(_silu(c_ref[...]), w_ref[...]) + b_ref[...]


def adaln(c, w_ada, b_ada):
    n, d = c.shape
    e = w_ada.shape[1]
    tn = 512
    return pl.pallas_call(
        _adaln_kernel,
        grid=(e // tn,),
        in_specs=[_const_spec((n, d)), pl.BlockSpec((d, tn), lambda j: (0, j)), pl.BlockSpec((1, tn), lambda j: (0, j))],
        out_specs=pl.BlockSpec((n, tn), lambda j: (0, j)),
        out_shape=jax.ShapeDtypeStruct((n, e), F32),
        compiler_params=_params(("arbitrary",)),
        name="adaln",
    )(c, w_ada, b_ada.reshape(1, e))


W1_COLS = Q_LORA + KV_LORA + 2 * LANES
WQ_COLS = H_B * QK_NOPE + 2 * H_B * LANES
WG_COLS = 2 * H_A * DK_A + H_A * DV_A + LANES


def _proj_kernel(x_ref, sc_ref, sh_ref, cos_ref, sin_ref, w1_ref, wq_ref, wuk_ref, wg_ref, wf2_ref, bf_ref,
                 gq_ref, gkv_ref,
                 ckv_ref, krope_ref, kcat_ref, kvt_ref, qcat_ref, qg_ref, kg_ref, vg_ref, logf_ref):
    gb, rb, d = x_ref.shape
    tm = gb * rb
    h = (x_ref[...] * (1.0 + sc_ref[...]) + sh_ref[...]).reshape(tm, d).astype(BF16)
    cos = cos_ref[...]
    sin = sin_ref[...]

    z1 = _dot(h, w1_ref[...])
    ckv = _rms(z1[:, Q_LORA:Q_LORA + KV_LORA], gkv_ref[...])
    o_kr = Q_LORA + KV_LORA
    krope = z1[:, o_kr:o_kr + LANES] * cos + z1[:, o_kr + LANES:o_kr + 2 * LANES] * sin
    ckv_ref[...] = ckv
    krope_ref[...] = krope[:, :QK_ROPE]
    kcat_ref[:, :KV_LORA] = ckv.astype(BF16)
    kcat_ref[:, KV_LORA:] = krope.astype(BF16)
    kvt_ref[...] = jnp.transpose(ckv).astype(BF16)

    qn = _rms(z1[:, :Q_LORA], gq_ref[...]).astype(BF16)
    q2 = _dot(qn, wq_ref[...])
    o_r = H_B * QK_NOPE
    o_s = o_r + H_B * LANES
    for hh in range(H_B):
        q_nope = q2[:, hh * QK_NOPE:(hh + 1) * QK_NOPE].astype(BF16)
        q_lat = _dot(q_nope, wuk_ref[hh])
        q_rope = q2[:, o_r + hh * LANES:o_r + (hh + 1) * LANES] * cos + q2[:, o_s + hh * LANES:o_s + (hh + 1) * LANES] * sin
        qcat_ref[0, hh, :, :KV_LORA] = (q_lat * Q_SCALE).astype(BF16)
        qcat_ref[0, hh, :, KV_LORA:] = (q_rope * Q_SCALE).astype(BF16)

    z2 = _dot(h, wg_ref[...])
    nk = H_A * DK_A
    qg_ref[...] = z2[:, :nk] * DK_A ** -0.5
    kg_ref[...] = z2[:, nk:2 * nk]
    vg_ref[...] = z2[:, 2 * nk:2 * nk + H_A * DV_A]
    fa = z2[:, 2 * nk + H_A * DV_A:].astype(BF16)
    f_pre = _dot(fa, wf2_ref[...]) + bf_ref[...]
    log_sig = jnp.minimum(f_pre, 0.0) - jnp.log(1.0 + jnp.exp(-jnp.abs(f_pre)))
    logf_ref[...] = log_sig / GLA_TAU


def _group_tiling(g, r):
    if r >= TOKEN_TILE:
        assert r % TOKEN_TILE == 0
        return 1, TOKEN_TILE
    assert TOKEN_TILE % r == 0 and g % (TOKEN_TILE // r) == 0 and r % 8 == 0
    return TOKEN_TILE // r, r


def _group_specs(g, r, d):
    gb, rb = _group_tiling(g, r)
    nr = r // rb
    x_spec = pl.BlockSpec((gb, rb, d), lambda i: (i // nr, i % nr, 0))
    mod_spec = pl.BlockSpec((gb, 1, d), lambda i: (i // nr, 0, 0))
    return x_spec, mod_spec, (g // gb) * nr


def _tok_spec(cols):
    return pl.BlockSpec((TOKEN_TILE, cols), lambda i: (i, 0))


def input_projections(x, sc, sh, cos_tbl, sin_tbl, pw):
    g, r, d = x.shape
    t = g * r
    x_spec, mod_spec, n_tiles = _group_specs(g, r, d)
    n_tbl = cos_tbl.shape[0] // TOKEN_TILE
    tbl_spec = pl.BlockSpec((TOKEN_TILE, LANES), lambda i: (i % n_tbl, 0))
    nk = H_A * DK_A
    head_cols = [(KV_LORA, F32), (QK_ROPE, F32), (QCAT, BF16)]
    tail_cols = [(nk, F32), (nk, F32), (H_A * DV_A, F32), (nk, F32)]
    mid_specs = [pl.BlockSpec((KV_LORA, TOKEN_TILE), lambda i: (0, i)),
                 pl.BlockSpec((1, H_B, TOKEN_TILE, QCAT), lambda i: (i, 0, 0, 0))]
    mid_shapes = [jax.ShapeDtypeStruct((KV_LORA, t), BF16), jax.ShapeDtypeStruct((n_tiles, H_B, TOKEN_TILE, QCAT), BF16)]
    return pl.pallas_call(
        _proj_kernel,
        grid=(n_tiles,),
        in_specs=[x_spec, mod_spec, mod_spec, tbl_spec, tbl_spec,
                  _const_spec(pw["w1"].shape), _const_spec(pw["wq"].shape), _const_spec(pw["wuk"].shape),
                  _const_spec(pw["wg"].shape), _const_spec(pw["wf2"].shape), _const_spec(pw["bf"].shape),
                  _const_spec(pw["gq"].shape), _const_spec(pw["gkv"].shape)],
        out_specs=[_tok_spec(c) for c, _ in head_cols] + mid_specs + [_tok_spec(c) for c, _ in tail_cols],
        out_shape=[jax.ShapeDtypeStruct((t, c), dt) for c, dt in head_cols] + mid_shapes
                  + [jax.ShapeDtypeStruct((t, c), dt) for c, dt in tail_cols],
        compiler_params=_params(("arbitrary",)),
        name="input_proj",
    )(x, sc, sh, cos_tbl, sin_tbl, pw["w1"], pw["wq"], pw["wuk"], pw["wg"], pw["wf2"], pw["bf"], pw["gq"], pw["gkv"])


def _flash_kernel(qi_ref, ki_ref, q_ref, k_ref, vt_ref, o_ref, m_sc, l_sc, acc_sc, *, tq, tk, hb):
    step = pl.program_id(1)
    qi = qi_ref[step]
    ki = ki_ref[step]
    cb = hb * tq

    @pl.when(ki == 0)
    def _():
        m_sc[...] = jnp.full(m_sc.shape, -jnp.inf, F32)
        l_sc[...] = jnp.zeros(l_sc.shape, F32)
        acc_sc[...] = jnp.zeros(acc_sc.shape, F32)

    def update(masked):
        k = k_ref[...]
        vt = vt_ref[...]
        for j in range(H_B // hb):
            cs = slice(j * cb, (j + 1) * cb)
            q = q_ref[0, j * hb:(j + 1) * hb].reshape(cb, QCAT)
            st = _dot(k, q, NT)
            if masked:
                key = ki * tk + lax.broadcasted_iota(jnp.int32, (tk, cb), 0)
                tok = qi * tq + lax.broadcasted_iota(jnp.int32, (tk, cb), 1) % tq
                st = jnp.where(key <= tok, st, -jnp.inf)
            m_prev = m_sc[:, cs]
            m_new = jnp.maximum(m_prev, jnp.max(st, axis=0, keepdims=True))
            corr = jnp.exp2(m_prev - m_new)
            p = jnp.exp2(st - m_new)
            l_sc[:, cs] = l_sc[:, cs] * corr + jnp.sum(p, axis=0, keepdims=True)
            acc_sc[:, cs] = acc_sc[:, cs] * corr + _dot(vt, p.astype(BF16))
            m_sc[:, cs] = m_new

    crosses_diagonal = (ki + 1) * tk - 1 > qi * tq

    @pl.when(crosses_diagonal)
    def _():
        update(True)

    @pl.when(jnp.logical_not(crosses_diagonal))
    def _():
        update(False)

    @pl.when((ki + 1) * tk >= (qi + 1) * tq)
    def _():
        for hh in range(H_B):
            cs = slice(hh * tq, (hh + 1) * tq)
            o_ref[0, hh] = jnp.transpose(acc_sc[:, cs] / l_sc[:, cs]).astype(o_ref.dtype)


def mla_prompt_attention(qcat, kcat, kvt, n_seq, seq):
    tq, tk = FLASH_TQ, min(FLASH_TK, seq)
    assert seq % tq == 0 and seq % tk == 0 and qcat.shape[2] == tq
    nq, nkb = seq // tq, seq // tk
    rows = tq * H_B
    qi_list, ki_list = [], []
    for qi in range(nq):
        for ki in range(-(-((qi + 1) * tq) // tk)):
            qi_list.append(qi)
            ki_list.append(ki)
    qi_tbl = jnp.asarray(np.array(qi_list, np.int32))
    ki_tbl = jnp.asarray(np.array(ki_list, np.int32))
    grid_spec = pltpu.PrefetchScalarGridSpec(
        num_scalar_prefetch=2,
        grid=(n_seq, len(qi_list)),
        in_specs=[pl.BlockSpec((1, H_B, tq, QCAT), lambda b, s, qt, kt: (b * nq + qt[s], 0, 0, 0)),
                  pl.BlockSpec((tk, QCAT), lambda b, s, qt, kt: (b * nkb + kt[s], 0)),
                  pl.BlockSpec((KV_LORA, tk), lambda b, s, qt, kt: (0, b * nkb + kt[s]))],
        out_specs=pl.BlockSpec((1, H_B, tq, KV_LORA), lambda b, s, qt, kt: (b * nq + qt[s], 0, 0, 0)),
        scratch_shapes=[pltpu.VMEM((1, rows), F32), pltpu.VMEM((1, rows), F32), pltpu.VMEM((KV_LORA, rows), F32)],
    )
    return pl.pallas_call(
        functools.partial(_flash_kernel, tq=tq, tk=tk, hb=FLASH_HEADS_PER_BLOCK),
        grid_spec=grid_spec,
        out_shape=jax.ShapeDtypeStruct((n_seq * nq, H_B, tq, KV_LORA), BF16),
        compiler_params=_params(("arbitrary", "arbitrary")),
        name="mla_prompt_attn",
    )(qi_tbl, ki_tbl, qcat, kcat, kvt)


PAGES_PER_CHUNK = 16


def _paged_kernel(pt_ref, q_ref, knew_ref, ckv_hbm, ckrt_hbm, o_ref, kv_buf, krt_buf, sems, *,
                  n_new, layer, page, chunk_pages, n_chunks):
    b = pl.program_id(0)
    n_b = pl.num_programs(0)
    rows = n_new * H_B

    def page_copies(seq, chunk, slot, j):
        phys = pt_ref[seq, chunk * chunk_pages + j]
        dst = pl.ds(pl.multiple_of(j * page, page), page)
        dst_t = pl.ds(pl.multiple_of(j * QK_ROPE, QK_ROPE), QK_ROPE)
        return (pltpu.make_async_copy(ckv_hbm.at[layer, phys], kv_buf.at[slot, dst], sems.at[0, slot]),
                pltpu.make_async_copy(ckrt_hbm.at[layer, phys], krt_buf.at[slot, dst_t], sems.at[1, slot]))

    def start_chunk(seq, chunk, slot):
        def body(j, carry):
            for cp in page_copies(seq, chunk, slot, j):
                cp.start()
            return carry
        lax.fori_loop(0, chunk_pages, body, 0)

    def wait_chunk(seq, chunk, slot):
        def body(j, carry):
            for cp in page_copies(seq, chunk, slot, j):
                cp.wait()
            return carry
        lax.fori_loop(0, chunk_pages, body, 0)

    @pl.when(b == 0)
    def _():
        start_chunk(0, 0, 0)

    q = q_ref[0]
    qf = q.astype(F32)
    kn = knew_ref[0].astype(F32)
    tok = lax.broadcasted_iota(jnp.int32, (rows, 1), 0) // H_B
    cols = []
    for j in range(n_new):
        sj = jnp.sum(qf * kn[j:j + 1, :], axis=-1, keepdims=True)
        cols.append(jnp.where(j <= tok, sj, -jnp.inf))
    m = cols[0]
    for j in range(1, n_new):
        m = jnp.maximum(m, cols[j])
    l = jnp.zeros((rows, 1), F32)
    acc = jnp.zeros((rows, KV_LORA), F32)
    for j in range(n_new):
        pj = jnp.exp2(cols[j] - m)
        l = l + pj
        acc = acc + pj * kn[j:j + 1, :KV_LORA]

    q_lat = q[:, :KV_LORA]
    q_rope = q[:, KV_LORA:KV_LORA + QK_ROPE]
    for c in range(n_chunks):
        slot = c % 2
        if c + 1 < n_chunks:
            start_chunk(b, c + 1, 1 - slot)
        else:
            @pl.when(b + 1 < n_b)
            def _():
                start_chunk(b + 1, 0, 1 - slot)
        wait_chunk(b, c, slot)
        kv = kv_buf[slot].astype(BF16)
        s_rope = [_dot(q_rope, krt_buf[slot, j * QK_ROPE:(j + 1) * QK_ROPE, :].astype(BF16)) for j in range(chunk_pages)]
        s = _dot(q_lat, kv, NT) + jnp.concatenate(s_rope, axis=-1)
        m_new = jnp.maximum(m, jnp.max(s, axis=-1, keepdims=True))
        corr = jnp.exp2(m - m_new)
        p = jnp.exp2(s - m_new)
        l = l * corr + jnp.sum(p, axis=-1, keepdims=True)
        acc = acc * corr + _dot(p.astype(BF16), kv)
        m = m_new
    o_ref[0] = (acc / l).astype(o_ref.dtype)


def mla_sample_attention(qcat, kcat, cache_kv, cache_krt, page_table, layer, n_seq, n_new):
    n_pages = page_table.shape[1]
    page = cache_kv.shape[2]
    rows = n_new * H_B
    chunk_pages = min(PAGES_PER_CHUNK, n_pages // 2)
    assert n_pages % (2 * chunk_pages) == 0
    n_chunks = n_pages // chunk_pages
    q3 = qcat.reshape(n_seq, rows, QCAT)
    k3 = kcat.reshape(n_seq, n_new, QCAT)
    grid_spec = pltpu.PrefetchScalarGridSpec(
        num_scalar_prefetch=1,
        grid=(n_seq,),
        in_specs=[pl.BlockSpec((1, rows, QCAT), lambda b, pt: (b, 0, 0)),
                  pl.BlockSpec((1, n_new, QCAT), lambda b, pt: (b, 0, 0)),
                  pl.BlockSpec(memory_space=pl.ANY),
                  pl.BlockSpec(memory_space=pl.ANY)],
        out_specs=pl.BlockSpec((1, rows, KV_LORA), lambda b, pt: (b, 0, 0)),
        scratch_shapes=[pltpu.VMEM((2, chunk_pages * page, KV_LORA), F32),
                        pltpu.VMEM((2, chunk_pages * QK_ROPE, page), F32),
                        pltpu.SemaphoreType.DMA((2, 2))],
    )
    o = pl.pallas_call(
        functools.partial(_paged_kernel, n_new=n_new, layer=layer, page=page, chunk_pages=chunk_pages,
                          n_chunks=n_chunks),
        grid_spec=grid_spec,
        out_shape=jax.ShapeDtypeStruct((n_seq, rows, KV_LORA), BF16),
        compiler_params=_params(("arbitrary",)),
        name="mla_sample_attn",
    )(page_table, q3, k3, cache_kv, cache_krt)
    return o.reshape(n_seq * n_new, H_B * KV_LORA)


def _gla_cumsum_matrix(chunk):
    blocks = [np.tril(np.ones((chunk, chunk), np.float32))]
    m = chunk // 2
    while m >= 1:
        w = np.zeros((chunk, chunk), np.float32)
        for t in range(chunk):
            seg = (t // m) * m
            if (t % (2 * m)) >= m:
                w[t, seg:t + 1] = 1.0
            else:
                w[t, t + 1:seg + m] = 1.0
        blocks.append(w)
        m //= 2
    return np.concatenate(blocks, axis=0)


def _gla_kernel(q_ref, k_ref, v_ref, g_ref, wc_ref, s0_ref, o_ref, sfin_ref, s_sc, *, chunk):
    c_idx = pl.program_id(1)

    @pl.when(c_idx == 0)
    def _():
        s_sc[...] = s0_ref[0]

    n_lev = chunk.bit_length() - 1
    row = lax.broadcasted_iota(jnp.int32, (chunk, chunk), 0)
    col = lax.broadcasted_iota(jnp.int32, (chunk, chunk), 1)
    row_in_chunk = lax.broadcasted_iota(jnp.int32, (chunk, DK_A), 0)
    wc = wc_ref[...]
    for hh in range(H_A):
        q = q_ref[:, hh * DK_A:(hh + 1) * DK_A]
        k = k_ref[:, hh * DK_A:(hh + 1) * DK_A]
        v = v_ref[:, hh * DV_A:(hh + 1) * DV_A].astype(BF16)
        g = g_ref[:, hh * DK_A:(hh + 1) * DK_A]
        sums = _dot_hi_exact_lhs(wc, g)
        b = sums[:chunk]
        b_last = b[chunk - 1:chunk, :]
        s_prev = s_sc[hh]

        a = jnp.where(row == col, _dot(q.astype(BF16), k.astype(BF16), NT), 0.0)
        for lev in range(n_lev):
            m = chunk >> (lev + 1)
            scale = jnp.exp(sums[(lev + 1) * chunk:(lev + 2) * chunk])
            upper = (row_in_chunk % (2 * m)) >= m
            q_l = jnp.where(upper, q * scale, 0.0).astype(BF16)
            k_l = jnp.where(upper, 0.0, k * scale).astype(BF16)
            same_block = (row // (2 * m)) == (col // (2 * m))
            a = a + jnp.where(same_block, _dot(q_l, k_l, NT), 0.0)

        inter = _dot((q * jnp.exp(b)).astype(BF16), s_prev.astype(BF16))
        o_ref[:, hh * DV_A:(hh + 1) * DV_A] = inter + _dot(a.astype(BF16), v)

        k_dec = (k * jnp.exp(b_last - b)).astype(BF16)
        decay_col = jnp.transpose(jnp.broadcast_to(jnp.exp(b_last), (DK_A, DK_A)))[:, :1]
        s_sc[hh] = decay_col * s_prev + _dot(k_dec, v, TN)

    @pl.when(c_idx == pl.num_programs(1) - 1)
    def _():
        sfin_ref[0] = s_sc[...]


def gla(qg, kg, vg, logf, s0, n_seq, seq, chunk):
    assert seq % chunk == 0 and chunk & (chunk - 1) == 0 and chunk % 8 == 0
    n_chunks = seq // chunk
    wc = jnp.asarray(_gla_cumsum_matrix(chunk), BF16)
    nk = H_A * DK_A

    def tok(cols):
        return pl.BlockSpec((chunk, cols), lambda b, c: (b * n_chunks + c, 0))

    state_spec = pl.BlockSpec((1, H_A, DK_A, DV_A), lambda b, c: (b, 0, 0, 0))
    return pl.pallas_call(
        functools.partial(_gla_kernel, chunk=chunk),
        grid=(n_seq, n_chunks),
        in_specs=[tok(nk), tok(nk), tok(H_A * DV_A), tok(nk), _const_spec(wc.shape), state_spec],
        out_specs=[tok(H_A * DV_A), state_spec],
        out_shape=[jax.ShapeDtypeStruct((n_seq * seq, H_A * DV_A), F32),
                   jax.ShapeDtypeStruct((n_seq, H_A, DK_A, DV_A), F32)],
        scratch_shapes=[pltpu.VMEM((H_A, DK_A, DV_A), F32)],
        compiler_params=_params(("arbitrary", "arbitrary")),
        name="gla",
    )(qg, kg, vg, logf, wc, s0)


def _route(h2, wrt_ref, rbias_ref, tm):
    logits = _dot_hi(wrt_ref[...], h2, NT)
    scores = _sigmoid(logits)
    sel = scores + rbias_ref[...]
    neg = -jnp.inf

    iota_g = lax.broadcasted_iota(jnp.int32, (GROUP_SIZE, tm), 0).astype(F32)
    g_rows = []
    for g in range(N_GROUPS):
        blk = sel[g * GROUP_SIZE:(g + 1) * GROUP_SIZE]
        m1 = jnp.max(blk, axis=0, keepdims=True)
        first = jnp.min(jnp.where(blk == m1, iota_g, float(GROUP_SIZE)), axis=0, keepdims=True)
        m2 = jnp.max(jnp.where(iota_g == first, neg, blk), axis=0, keepdims=True)
        g_rows.append(m1 + m2)
    g_score = jnp.concatenate(g_rows, axis=0)

    iota_n = lax.broadcasted_iota(jnp.int32, (N_GROUPS, tm), 0).astype(F32)
    g_keep = jnp.zeros((N_GROUPS, tm), F32)
    for _ in range(TOPK_GROUPS):
        mx = jnp.max(g_score, axis=0, keepdims=True)
        first = jnp.min(jnp.where(g_score == mx, iota_n, float(N_GROUPS)), axis=0, keepdims=True)
        hit = iota_n == first
        g_keep = jnp.where(hit, 1.0, g_keep)
        g_score = jnp.where(hit, neg, g_score)

    sel_m = jnp.concatenate(
        [jnp.where(g_keep[g:g + 1] > 0.0, sel[g * GROUP_SIZE:(g + 1) * GROUP_SIZE], neg) for g in range(N_GROUPS)], axis=0)

    iota_e = lax.broadcasted_iota(jnp.int32, (N_EXPERTS, tm), 0).astype(F32)
    ids, wts, hits = [], [], []
    for _ in range(TOP_K):
        mx = jnp.max(sel_m, axis=0, keepdims=True)
        first = jnp.min(jnp.where(sel_m == mx, iota_e, float(N_EXPERTS)), axis=0, keepdims=True)
        hit = iota_e == first
        ids.append(first)
        hits.append(hit)
        wts.append(jnp.sum(jnp.where(hit, scores, 0.0), axis=0, keepdims=True))
        sel_m = jnp.where(hit, neg, sel_m)
    ids = jnp.concatenate(ids, axis=0)
    wts = jnp.concatenate(wts, axis=0)
    wts = wts / jnp.sum(wts, axis=0, keepdims=True) * ROUTED_SCALE
    return ids, wts, hits


def _rank_in_expert(hits, cnt_sc, tm):
    chosen = jnp.where(hits[0], 1.0, 0.0)
    for hit in hits[1:]:
        chosen = jnp.where(hit, 1.0, chosen)
    earlier = (lax.broadcasted_iota(jnp.int32, (tm, tm), 0) < lax.broadcasted_iota(jnp.int32, (tm, tm), 1))
    prefix = _dot(chosen.astype(BF16), jnp.where(earlier, 1.0, 0.0).astype(BF16))
    base = cnt_sc[:, :1] + prefix
    ranks = [jnp.sum(jnp.where(hit, base, 0.0), axis=0, keepdims=True) for hit in hits]
    cnt_sc[...] = cnt_sc[...] + jnp.sum(chosen, axis=1, keepdims=True)
    return jnp.concatenate(ranks, axis=0)


def _merge_kernel(x_ref, scm_ref, shm_ref, gm_ref, scf_ref, shf_ref, olat_ref, ogla_ref,
                  w3_ref, wuv_ref, wb_ref, wa_ref, wo_ref, ggla_ref, ln1g_ref, ln1b_ref, wrt_ref, rbias_ref, cnt0_ref,
                  x1_ref, h2_ref, ids_ref, wts_ref, rank_ref, cnt_ref, cnt_sc, *, alpha):
    @pl.when(pl.program_id(0) == 0)
    def _():
        cnt_sc[...] = cnt0_ref[...]

    gb, rb, d = x_ref.shape
    tm = gb * rb
    x = x_ref[...]
    h = (x * (1.0 + scm_ref[...]) + shm_ref[...]).reshape(tm, d).astype(BF16)
    z3 = _dot(h, w3_ref[...])

    vb = [_dot(olat_ref[0, hh], wuv_ref[hh]).astype(BF16) for hh in range(H_B)]
    y_b = _dot(jnp.concatenate(vb, axis=-1), wb_ref[...])

    ga = []
    for hh in range(H_A):
        o_n = _rms(ogla_ref[:, hh * DV_A:(hh + 1) * DV_A], ggla_ref[...])
        ga.append((o_n * _silu(z3[:, hh * DV_A:(hh + 1) * DV_A])).astype(BF16))
    y_a = _dot(jnp.concatenate(ga, axis=-1), wa_ref[...])

    merged = _sigmoid(z3[:, d:2 * d]) * y_a + _sigmoid(z3[:, 2 * d:]) * y_b
    mix = _dot(merged.astype(BF16), wo_ref[...]).reshape(gb, rb, d)
    x1 = _layer_norm(alpha * x + gm_ref[...] * mix, ln1g_ref[...], ln1b_ref[...])
    x1_ref[...] = x1
    h2 = (x1 * (1.0 + scf_ref[...]) + shf_ref[...]).reshape(tm, d)
    h2_ref[...] = h2.astype(BF16)
    ids, wts, hits = _route(h2, wrt_ref, rbias_ref, tm)
    ids_ref[...] = ids.astype(jnp.int32)
    wts_ref[...] = wts
    rank_ref[...] = _rank_in_expert(hits, cnt_sc, tm).astype(jnp.int32)
    cnt_ref[...] = cnt_sc[...]


def merge_and_route(x, mods, olat, ogla, mw, cnt0, alpha):
    g, r, d = x.shape
    t = g * r
    x_spec, mod_spec, n_tiles = _group_specs(g, r, d)
    names = ["w3", "wuv", "wb", "wa", "wo", "ggla", "ln1g", "ln1b", "wrt", "rbias"]
    kt_spec = pl.BlockSpec((TOP_K, TOKEN_TILE), lambda i: (0, i))
    cnt_spec = _const_spec((N_EXPERTS, LANES))
    return pl.pallas_call(
        functools.partial(_merge_kernel, alpha=alpha),
        grid=(n_tiles,),
        in_specs=[x_spec] + [mod_spec] * 5
                 + [pl.BlockSpec((1, H_B, TOKEN_TILE, KV_LORA), lambda i: (i, 0, 0, 0)), _tok_spec(H_A * DV_A)]
                 + [_const_spec(mw[n].shape) for n in names] + [cnt_spec],
        out_specs=[x_spec, _tok_spec(d), kt_spec, kt_spec, kt_spec, cnt_spec],
        out_shape=[jax.ShapeDtypeStruct((g, r, d), F32), jax.ShapeDtypeStruct((t, d), BF16),
                   jax.ShapeDtypeStruct((TOP_K, t), jnp.int32), jax.ShapeDtypeStruct((TOP_K, t), F32),
                   jax.ShapeDtypeStruct((TOP_K, t), jnp.int32), jax.ShapeDtypeStruct((N_EXPERTS, LANES), F32)],
        scratch_shapes=[pltpu.VMEM((N_EXPERTS, LANES), F32)],
        compiler_params=_params(("arbitrary",)),
        name="merge_route",
    )(x, *mods, olat, ogla, *[mw[n] for n in names], cnt0)


def _moe_kernel(blk_e_ref, n_used_ref, x_ref, wg_ref, wu_ref, wd_ref, y_ref, wg_sc, wu_sc, wd_sc):
    i = pl.program_id(0)
    new_expert = jnp.logical_or(i == 0, blk_e_ref[i] != blk_e_ref[jnp.maximum(i - 1, 0)])

    @pl.when(new_expert)
    def _():
        wg_sc[...] = wg_ref[0].astype(BF16)
        wu_sc[...] = wu_ref[0].astype(BF16)
        wd_sc[...] = wd_ref[0].astype(BF16)

    @pl.when(i < n_used_ref[0])
    def _():
        x = x_ref[...]
        gate = _dot(x, wg_sc[...])
        up = _dot(x, wu_sc[...])
        act = (_silu(gate) * up).astype(BF16)
        y_ref[...] = _dot(act, wd_sc[...]).astype(y_ref.dtype)

    @pl.when(i >= n_used_ref[0])
    def _():
        y_ref[...] = jnp.zeros(y_ref.shape, y_ref.dtype)


def routed_expert_blocks(x_sorted, blk_e, n_used, w_gate, w_up, w_down):
    m_pad, d = x_sorted.shape
    n_blk = m_pad // MOE_ROWS
    de = w_gate.shape[-1]
    grid_spec = pltpu.PrefetchScalarGridSpec(
        num_scalar_prefetch=2,
        grid=(n_blk,),
        in_specs=[pl.BlockSpec((MOE_ROWS, d), lambda i, be, nu: (i, 0)),
                  pl.BlockSpec((1, d, de), lambda i, be, nu: (be[i], 0, 0)),
                  pl.BlockSpec((1, d, de), lambda i, be, nu: (be[i], 0, 0)),
                  pl.BlockSpec((1, de, d), lambda i, be, nu: (be[i], 0, 0))],
        out_specs=pl.BlockSpec((MOE_ROWS, d), lambda i, be, nu: (i, 0)),
        scratch_shapes=[pltpu.VMEM((d, de), BF16), pltpu.VMEM((d, de), BF16), pltpu.VMEM((de, d), BF16)],
    )
    return pl.pallas_call(
        _moe_kernel,
        grid_spec=grid_spec,
        out_shape=jax.ShapeDtypeStruct((m_pad, d), BF16),
        compiler_params=_params(("arbitrary",)),
        name="moe_experts",
    )(blk_e, n_used, x_sorted, w_gate, w_up, w_down)


def _pos_kernel(ids_ref, rank_ref, pstart_ref, pos_ref):
    k, tm = ids_ref.shape
    iota_e = lax.broadcasted_iota(jnp.int32, (N_EXPERTS, tm), 0)
    pstart = pstart_ref[:, :1]
    base = [jnp.sum(jnp.where(iota_e == ids_ref[j:j + 1, :], pstart, 0.0), axis=0, keepdims=True) for j in range(k)]
    pos_ref[...] = jnp.concatenate(base, axis=0).astype(jnp.int32) + rank_ref[...]


def dispatch(ids, rank, counts):
    k, t = ids.shape
    m = k * t
    counts = counts[:, 0].astype(jnp.int32)
    padded = (counts + MOE_ROWS - 1) // MOE_ROWS * MOE_ROWS
    pend = jnp.cumsum(padded)
    pstart = jnp.broadcast_to((pend - padded).astype(F32)[:, None], (N_EXPERTS, LANES))
    kt_spec = pl.BlockSpec((k, TOKEN_TILE), lambda i: (0, i))
    pos = pl.pallas_call(
        _pos_kernel,
        grid=(t // TOKEN_TILE,),
        in_specs=[kt_spec, kt_spec, _const_spec((N_EXPERTS, LANES))],
        out_specs=kt_spec,
        out_shape=jax.ShapeDtypeStruct((k, t), jnp.int32),
        compiler_params=_params(("arbitrary",)),
        name="dispatch_positions",
    )(ids, rank, pstart)
    m_pad = -(-(m + N_EXPERTS * (MOE_ROWS - 1)) // MOE_ROWS) * MOE_ROWS
    n_blk = m_pad // MOE_ROWS
    tok = jnp.broadcast_to(jnp.arange(t, dtype=jnp.int32)[None, :], (k, t))
    filler = jnp.arange(m_pad, dtype=jnp.int32) % t
    tok_pad = filler.at[pos.reshape(m)].set(tok.reshape(m), unique_indices=True, mode="promise_in_bounds")
    blk_e = jnp.minimum(jnp.searchsorted(pend, jnp.arange(n_blk) * MOE_ROWS, side="right"), N_EXPERTS - 1).astype(jnp.int32)
    n_used = (pend[-1] // MOE_ROWS).astype(jnp.int32).reshape(1)
    return tok_pad, pos, blk_e, n_used


def _final_kernel(x1_ref, gf_ref, h2_ref, yg_ref, wts_ref, wsg_ref, wsu_ref, wsd_ref, ln2g_ref, ln2b_ref, y_ref, *, alpha):
    gb, rb, d = x1_ref.shape
    h2 = h2_ref[...]
    act = (_silu(_dot(h2, wsg_ref[...])) * _dot(h2, wsu_ref[...])).astype(BF16)
    ffn = _dot(act, wsd_ref[...])
    w = wts_ref[...]
    routed = yg_ref[0].astype(F32) * w[:, 0:1]
    for j in range(1, TOP_K):
        routed = routed + yg_ref[j].astype(F32) * w[:, j:j + 1]
    ffn = (ffn + routed).reshape(gb, rb, d)
    y_ref[...] = _layer_norm(alpha * x1_ref[...] + gf_ref[...] * ffn, ln2g_ref[...], ln2b_ref[...])


def shared_combine_norm(x1, gf, h2, yg, wts_t, fw, alpha):
    g, r, d = x1.shape
    x_spec, mod_spec, n_tiles = _group_specs(g, r, d)
    names = ["wsg", "wsu", "wsd", "ln2g", "ln2b"]
    return pl.pallas_call(
        functools.partial(_final_kernel, alpha=alpha),
        grid=(n_tiles,),
        in_specs=[x_spec, mod_spec, _tok_spec(d), pl.BlockSpec((TOP_K, TOKEN_TILE, d), lambda i: (0, i, 0)),
                  _tok_spec(TOP_K)] + [_const_spec(fw[n].shape) for n in names],
        out_specs=x_spec,
        out_shape=jax.ShapeDtypeStruct((g, r, d), F32),
        compiler_params=_params(("arbitrary",)),
        name="shared_combine_norm",
    )(x1, gf, h2, yg, wts_t, *[fw[n] for n in names])


def _rot_cols(w):
    half = w.shape[-1] // 2
    return jnp.concatenate([-w[..., half:], w[..., :half]], axis=-1)


def _pad_cols(w, width):
    return jnp.pad(w, [(0, 0)] * (w.ndim - 1) + [(0, width - w.shape[-1])])


def _prepare_weights(w):
    cuts = np.cumsum((0,) + IN_SIZES)
    part = [w["w_in"][:, cuts[i]:cuts[i + 1]] for i in range(len(IN_SIZES))]
    w_qa, w_kva, w_kr, w_gq, w_gk, w_gv, w_go, w_gf, w_ga, w_gb = part
    w1 = jnp.concatenate([w_qa, w_kva, _pad_cols(w_kr, LANES), _pad_cols(_rot_cols(w_kr), LANES)], axis=1)
    wuq = w["w_uq"].reshape(Q_LORA, H_B, QK_NOPE + QK_ROPE)
    wuq_rope = wuq[:, :, QK_NOPE:]
    wq = jnp.concatenate([wuq[:, :, :QK_NOPE].reshape(Q_LORA, H_B * QK_NOPE),
                          _pad_cols(wuq_rope, LANES).reshape(Q_LORA, H_B * LANES),
                          _pad_cols(_rot_cols(wuq_rope), LANES).reshape(Q_LORA, H_B * LANES)], axis=1)
    wg = jnp.concatenate([w_gq, w_gk, w_gv, _pad_cols(w_gf, LANES)], axis=1)
    proj = dict(
        w1=w1.astype(BF16), wq=wq.astype(BF16), wg=wg.astype(BF16),
        wuk=jnp.transpose(w["w_uk"], (1, 2, 0)).astype(BF16),
        wf2=jnp.pad(w["w_gla_f2"], ((0, LANES - GLA_LR), (0, 0))).astype(BF16),
        bf=w["b_gla_f"].reshape(1, -1), gq=w["g_q_norm"].reshape(1, -1), gkv=w["g_kv_norm"].reshape(1, -1))
    merge = dict(
        w3=jnp.concatenate([w_go, w_ga, w_gb], axis=1).astype(BF16),
        wuv=jnp.transpose(w["w_uv"], (1, 0, 2)).astype(BF16),
        wb=w["w_b_out"].astype(BF16), wa=w["w_a_out"].astype(BF16), wo=w["w_o"].astype(BF16),
        ggla=w["g_gla_norm"].reshape(1, -1), ln1g=w["ln1_g"].reshape(1, -1), ln1b=w["ln1_b"].reshape(1, -1),
        wrt=jnp.transpose(w["w_router"]), rbias=w["router_bias"].reshape(-1, 1))
    final = dict(
        wsg=w["w_s_gate"].astype(BF16), wsu=w["w_s_up"].astype(BF16), wsd=w["w_s_down"].astype(BF16),
        ln2g=w["ln2_g"].reshape(1, -1), ln2b=w["ln2_b"].reshape(1, -1))
    return proj, merge, final


def _rope_tables(pos, tile_rows):
    half = QK_ROPE // 2
    freqs = ROPE_THETA ** (-jnp.arange(half, dtype=F32) / half)
    ang = pos.astype(F32)[:, None] * freqs
    cos = _pad_cols(jnp.concatenate([jnp.cos(ang)] * 2, axis=-1), LANES)
    sin = _pad_cols(jnp.concatenate([jnp.sin(ang)] * 2, axis=-1), LANES)
    if pos.shape[0] < tile_rows:
        rep = tile_rows // pos.shape[0]
        cos, sin = jnp.tile(cos, (rep, 1)), jnp.tile(sin, (rep, 1))
    return cos, sin


def _split_mods(mod, n):
    return [m.reshape(n, 1, D_MODEL) for m in jnp.split(mod, 6, axis=-1)]


def kernel(x_prompt, x_sample, c_prompt, c_sample, cache_kv_latent, cache_k_rope, state_gla, page_table, w_ada, b_ada, w_in, g_q_norm, w_uq, g_kv_norm, w_uk, w_uv, w_gla_f2, b_gla_f, g_gla_norm, w_a_out, w_b_out, w_o, ln1_g, ln1_b, w_router, router_bias, w_e_gate, w_e_up, w_e_down, w_s_gate, w_s_up, w_s_down, ln2_g, ln2_b):
    w_all = dict(w_ada=w_ada, b_ada=b_ada, w_in=w_in, g_q_norm=g_q_norm, w_uq=w_uq, g_kv_norm=g_kv_norm, w_uk=w_uk,
                 w_uv=w_uv, w_gla_f2=w_gla_f2, b_gla_f=b_gla_f, g_gla_norm=g_gla_norm, w_a_out=w_a_out,
                 w_b_out=w_b_out, w_o=w_o, ln1_g=ln1_g, ln1_b=ln1_b, w_router=w_router, router_bias=router_bias,
                 w_e_gate=w_e_gate, w_e_up=w_e_up, w_e_down=w_e_down, w_s_gate=w_s_gate, w_s_up=w_s_up,
                 w_s_down=w_s_down, ln2_g=ln2_g, ln2_b=ln2_b)
    depth = w_ada.shape[0]
    alpha = (2.0 * depth) ** 0.25
    n_p, s_p, _ = x_prompt.shape
    n_s, s_s, _ = x_sample.shape
    t_p, t_s = n_p * s_p, n_s * s_s
    past_len = page_table.shape[1] * cache_kv_latent.shape[2]
    cache_krt = jnp.swapaxes(cache_k_rope, 2, 3)
    cos_p, sin_p = _rope_tables(jnp.arange(s_p, dtype=jnp.int32), TOKEN_TILE)
    cos_s, sin_s = _rope_tables(past_len + jnp.arange(s_s, dtype=jnp.int32), TOKEN_TILE)

    n_c = n_p + n_s
    n_c_pad = -(-n_c // 16) * 16
    c_all = jnp.pad(jnp.concatenate([c_prompt, c_sample], axis=0), ((0, n_c_pad - n_c), (0, 0)))

    yp, ys = x_prompt, x_sample
    outs = [[] for _ in range(6)]
    for layer in range(depth):
        w = {name: arr[layer] for name, arr in w_all.items()}
        pw, mw, fw = _prepare_weights(w)
        mod = adaln(c_all, w["w_ada"], w["b_ada"])
        mods_p = _split_mods(mod[:n_p], n_p)
        mods_s = _split_mods(mod[n_p:n_c], n_s)

        ckv_p, kr_p, kcat_p, kvt_p, qcat_p, qg_p, kg_p, vg_p, lf_p = input_projections(
            yp, mods_p[1], mods_p[0], cos_p, sin_p, pw)
        olat_p = mla_prompt_attention(qcat_p, kcat_p, kvt_p, n_p, s_p)
        s0_p = jnp.zeros((n_p, H_A, DK_A, DV_A), F32)
        ogla_p, sfin_p = gla(qg_p, kg_p, vg_p, lf_p, s0_p, n_p, s_p, min(GLA_CHUNK_PROMPT, s_p))
        cnt0 = jnp.zeros((N_EXPERTS, LANES), F32)
        x1_p, h2_p, ids_p, wts_p, rank_p, cnt_p = merge_and_route(
            yp, [mods_p[1], mods_p[0], mods_p[2], mods_p[4], mods_p[3]], olat_p, ogla_p, mw, cnt0, alpha)

        ckv_s, kr_s, kcat_s, _, qcat_s, qg_s, kg_s, vg_s, lf_s = input_projections(
            ys, mods_s[1], mods_s[0], cos_s, sin_s, pw)
        qtok_s = jnp.transpose(qcat_s, (0, 2, 1, 3)).reshape(t_s, H_B * QCAT)
        otok_s = mla_sample_attention(qtok_s, kcat_s, cache_kv_latent, cache_krt, page_table, layer, n_s, s_s)
        olat_s = jnp.transpose(otok_s.reshape(t_s // TOKEN_TILE, TOKEN_TILE, H_B, KV_LORA), (0, 2, 1, 3))
        ogla_s, sfin_s = gla(qg_s, kg_s, vg_s, lf_s, state_gla[layer], n_s, s_s, s_s)
        x1_s, h2_s, ids_s, wts_s, rank_s, cnt = merge_and_route(
            ys, [mods_s[1], mods_s[0], mods_s[2], mods_s[4], mods_s[3]], olat_s, ogla_s, mw, cnt_p, alpha)

        h2 = jnp.concatenate([h2_p, h2_s], axis=0)
        ids = jnp.concatenate([ids_p, ids_s], axis=1)
        rank = jnp.concatenate([rank_p, rank_s], axis=1)
        tok_pad, pos, blk_e, n_used = dispatch(ids, rank, cnt)
        x_sorted = h2.at[tok_pad].get(mode="promise_in_bounds")
        y_sorted = routed_expert_blocks(x_sorted, blk_e, n_used, w["w_e_gate"], w["w_e_up"], w["w_e_down"])
        yg_p = y_sorted.at[pos[:, :t_p]].get(mode="promise_in_bounds")
        yg_s = y_sorted.at[pos[:, t_p:]].get(mode="promise_in_bounds")

        yp = shared_combine_norm(x1_p, mods_p[5], h2_p, yg_p, jnp.transpose(wts_p), fw, alpha)
        ys = shared_combine_norm(x1_s, mods_s[5], h2_s, yg_s, jnp.transpose(wts_s), fw, alpha)

        for lst, val in zip(outs, [ckv_p.reshape(n_p, s_p, KV_LORA), kr_p.reshape(n_p, s_p, QK_ROPE), sfin_p,
                                   ckv_s.reshape(n_s, s_s, KV_LORA), kr_s.reshape(n_s, s_s, QK_ROPE), sfin_s]):
            lst.append(val.astype(state_gla.dtype) if val.ndim == 4 else val)
    return (yp, ys) + tuple(jnp.stack(o) for o in outs)
```

```python
import functools
import math

import numpy as np
import jax
import jax.numpy as jnp
from jax import lax
from jax.experimental import pallas as pl
from jax.experimental.pallas import tpu as pltpu

F32 = jnp.float32
BF16 = jnp.bfloat16

D_MODEL = 1024
H_A, DK_A, DV_A, GLA_LR, GLA_TAU = 4, 128, 256, 16, 16.0
H_B, Q_LORA, KV_LORA, QK_NOPE, QK_ROPE, V_HEAD = 8, 384, 256, 128, 64, 128
ROPE_THETA = 10000.0
ATTN_SCALE = (QK_NOPE + QK_ROPE) ** -0.5
Q_SCALE = ATTN_SCALE * 1.4426950408889634
N_EXPERTS, TOP_K, N_GROUPS, TOPK_GROUPS = 256, 8, 8, 4
GROUP_SIZE = N_EXPERTS // N_GROUPS
D_EXPERT, D_SHARED, ROUTED_SCALE = 256, 256, 2.5
EPS = 1e-6
IN_SIZES = (Q_LORA, KV_LORA, QK_ROPE, H_A * DK_A, H_A * DK_A, H_A * DV_A, H_A * DV_A, GLA_LR, D_MODEL, D_MODEL)

LANES = 128
QCAT = KV_LORA + LANES
VMEM_LIMIT = 56 * 1024 * 1024

TOKEN_TILE = 256
FLASH_TQ, FLASH_TK = 512, 512
GLA_CHUNK_PROMPT = 128
MOE_ROWS = 256

NN = (((1,), (0,)), ((), ()))
NT = (((1,), (1,)), ((), ()))
TN = (((0,), (0,)), ((), ()))


def _dot(a, b, dims=NN):
    return lax.dot_general(a, b, dims, preferred_element_type=F32)


def _split3(x):
    x1 = x.astype(BF16)
    r1 = x - x1.astype(F32)
    x2 = r1.astype(BF16)
    x3 = (r1 - x2.astype(F32)).astype(BF16)
    return x1, x2, x3


def _dot_hi(a, b, dims=NN):
    a1, a2, a3 = _split3(a)
    b1, b2, b3 = _split3(b)
    small = _dot(a3, b1, dims) + _dot(a2, b2, dims) + _dot(a1, b3, dims)
    mid = _dot(a2, b1, dims) + _dot(a1, b2, dims)
    return (small + mid) + _dot(a1, b1, dims)


def _dot_hi3(a, b, dims=NN):
    a1 = a.astype(BF16)
    a2 = (a - a1.astype(F32)).astype(BF16)
    b1 = b.astype(BF16)
    b2 = (b - b1.astype(F32)).astype(BF16)
    return (_dot(a2, b1, dims) + _dot(a1, b2, dims)) + _dot(a1, b1, dims)


def _dot_hi_exact_lhs(w, x, dims=NN):
    x1, x2, x3 = _split3(x)
    return (_dot(w, x3, dims) + _dot(w, x2, dims)) + _dot(w, x1, dims)


def _sigmoid(x):
    return 1.0 / (1.0 + jnp.exp(-x))


def _silu(x):
    return x * _sigmoid(x)


def _rms(x, g):
    return x * lax.rsqrt(jnp.mean(x * x, axis=-1, keepdims=True) + EPS) * g


def _layer_norm(x, g, b):
    mu = jnp.mean(x, axis=-1, keepdims=True)
    xc = x - mu
    var = jnp.mean(xc * xc, axis=-1, keepdims=True)
    return xc * lax.rsqrt(var + EPS) * g + b


def _params(sem):
    return pltpu.CompilerParams(dimension_semantics=sem, vmem_limit_bytes=VMEM_LIMIT)


def _const_spec(shape):
    nd = len(shape)
    return pl.BlockSpec(shape, lambda *_: (0,) * nd)


def _adaln_kernel(c_ref, w_ref, b_ref, o_ref):
    o_ref[...] = _dot_hi(_silu(c_ref[...]), w_ref[...]) + b_ref[...]


def adaln(c, w_ada, b_ada):
    n, d = c.shape
    e = w_ada.shape[1]
    tn = 512
    return pl.pallas_call(
        _adaln_kernel,
        grid=(e // tn,),
        in_specs=[_const_spec((n, d)), pl.BlockSpec((d, tn), lambda j: (0, j)), pl.BlockSpec((1, tn), lambda j: (0, j))],
        out_specs=pl.BlockSpec((n, tn), lambda j: (0, j)),
        out_shape=jax.ShapeDtypeStruct((n, e), F32),
        compiler_params=_params(("arbitrary",)),
        name="adaln",
    )(c, w_ada, b_ada.reshape(1, e))


W1_COLS = Q_LORA + KV_LORA + 2 * LANES
WQ_COLS = H_B * QK_NOPE + 2 * H_B * LANES
WG_COLS = 2 * H_A * DK_A + H_A * DV_A + LANES


def _proj_kernel(x_ref, sc_ref, sh_ref, cos_ref, sin_ref, w1_ref, wq_ref, wuk_ref, wg_ref, wf2_ref, bf_ref,
                 gq_ref, gkv_ref,
                 ckv_ref, krope_ref, kcat_ref, kvt_ref, qcat_ref, qg_ref, kg_ref, vg_ref, logf_ref):
    gb, rb, d = x_ref.shape
    tm = gb * rb
    h = (x_ref[...] * (1.0 + sc_ref[...]) + sh_ref[...]).reshape(tm, d).astype(BF16)
    cos = cos_ref[...]
    sin = sin_ref[...]

    z1 = _dot(h, w1_ref[...])
    ckv = _rms(z1[:, Q_LORA:Q_LORA + KV_LORA], gkv_ref[...])
    o_kr = Q_LORA + KV_LORA
    krope = z1[:, o_kr:o_kr + LANES] * cos + z1[:, o_kr + LANES:o_kr + 2 * LANES] * sin
    ckv_ref[...] = ckv
    krope_ref[...] = krope[:, :QK_ROPE]
    kcat_ref[:, :KV_LORA] = ckv.astype(BF16)
    kcat_ref[:, KV_LORA:] = krope.astype(BF16)
    kvt_ref[...] = jnp.transpose(ckv).astype(BF16)

    qn = _rms(z1[:, :Q_LORA], gq_ref[...]).astype(BF16)
    q2 = _dot(qn, wq_ref[...])
    o_r = H_B * QK_NOPE
    o_s = o_r + H_B * LANES
    for hh in range(H_B):
        q_nope = q2[:, hh * QK_NOPE:(hh + 1) * QK_NOPE].astype(BF16)
        q_lat = _dot(q_nope, wuk_ref[hh])
        q_rope = q2[:, o_r + hh * LANES:o_r + (hh + 1) * LANES] * cos + q2[:, o_s + hh * LANES:o_s + (hh + 1) * LANES] * sin
        qcat_ref[0, hh, :, :KV_LORA] = (q_lat * Q_SCALE).astype(BF16)
        qcat_ref[0, hh, :, KV_LORA:] = (q_rope * Q_SCALE).astype(BF16)

    z2 = _dot(h, wg_ref[...])
    nk = H_A * DK_A
    qg_ref[...] = z2[:, :nk] * DK_A ** -0.5
    kg_ref[...] = z2[:, nk:2 * nk]
    vg_ref[...] = z2[:, 2 * nk:2 * nk + H_A * DV_A]
    fa = z2[:, 2 * nk + H_A * DV_A:].astype(BF16)
    f_pre = _dot(fa, wf2_ref[...]) + bf_ref[...]
    log_sig = jnp.minimum(f_pre, 0.0) - jnp.log(1.0 + jnp.exp(-jnp.abs(f_pre)))
    logf_ref[...] = log_sig / GLA_TAU


def _group_tiling(g, r):
    if r >= TOKEN_TILE:
        assert r % TOKEN_TILE == 0
        return 1, TOKEN_TILE
    assert TOKEN_TILE % r == 0 and g % (TOKEN_TILE // r) == 0 and r % 8 == 0
    return TOKEN_TILE // r, r


def _group_specs(g, r, d):
    gb, rb = _group_tiling(g, r)
    nr = r // rb
    x_spec = pl.BlockSpec((gb, rb, d), lambda i: (i // nr, i % nr, 0))
    mod_spec = pl.BlockSpec((gb, 1, d), lambda i: (i // nr, 0, 0))
    return x_spec, mod_spec, (g // gb) * nr


def _tok_spec(cols):
    return pl.BlockSpec((TOKEN_TILE, cols), lambda i: (i, 0))


def input_projections(x, sc, sh, cos_tbl, sin_tbl, pw):
    g, r, d = x.shape
    t = g * r
    x_spec, mod_spec, n_tiles = _group_specs(g, r, d)
    n_tbl = cos_tbl.shape[0] // TOKEN_TILE
    tbl_spec = pl.BlockSpec((TOKEN_TILE, LANES), lambda i: (i % n_tbl, 0))
    nk = H_A * DK_A
    head_cols = [(KV_LORA, F32), (QK_ROPE, F32), (QCAT, BF16)]
    tail_cols = [(nk, F32), (nk, F32), (H_A * DV_A, F32), (nk, F32)]
    mid_specs = [pl.BlockSpec((KV_LORA, TOKEN_TILE), lambda i: (0, i)),
                 pl.BlockSpec((1, H_B, TOKEN_TILE, QCAT), lambda i: (i, 0, 0, 0))]
    mid_shapes = [jax.ShapeDtypeStruct((KV_LORA, t), BF16), jax.ShapeDtypeStruct((n_tiles, H_B, TOKEN_TILE, QCAT), BF16)]
    return pl.pallas_call(
        _proj_kernel,
        grid=(n_tiles,),
        in_specs=[x_spec, mod_spec, mod_spec, tbl_spec, tbl_spec,
                  _const_spec(pw["w1"].shape), _const_spec(pw["wq"].shape), _const_spec(pw["wuk"].shape),
                  _const_spec(pw["wg"].shape), _const_spec(pw["wf2"].shape), _const_spec(pw["bf"].shape),
                  _const_spec(pw["gq"].shape), _const_spec(pw["gkv"].shape)],
        out_specs=[_tok_spec(c) for c, _ in head_cols] + mid_specs + [_tok_spec(c) for c, _ in tail_cols],
        out_shape=[jax.ShapeDtypeStruct((t, c), dt) for c, dt in head_cols] + mid_shapes
                  + [jax.ShapeDtypeStruct((t, c), dt) for c, dt in tail_cols],
        compiler_params=_params(("arbitrary",)),
        name="input_proj",
    )(x, sc, sh, cos_tbl, sin_tbl, pw["w1"], pw["wq"], pw["wuk"], pw["wg"], pw["wf2"], pw["bf"], pw["gq"], pw["gkv"])


def _flash_kernel(qi_ref, ki_ref, q_ref, k_ref, vt_ref, o_ref, m_sc, l_sc, acc_sc, *, tq, tk):
    step = pl.program_id(1)
    qi = qi_ref[step]
    ki = ki_ref[step]
    n_tiles, _, tt, _ = q_ref.shape
    cols = n_tiles * H_B * tt

    @pl.when(ki == 0)
    def _():
        m_sc[...] = jnp.full(m_sc.shape, -jnp.inf, F32)
        l_sc[...] = jnp.zeros(l_sc.shape, F32)
        acc_sc[...] = jnp.zeros(acc_sc.shape, F32)

    def update(masked):
        st = _dot(k_ref[...], q_ref[...].reshape(cols, QCAT), NT)
        if masked:
            key = ki * tk + lax.broadcasted_iota(jnp.int32, (tk, cols), 0)
            col = lax.broadcasted_iota(jnp.int32, (tk, cols), 1)
            tok = qi * tq + (col // (H_B * tt)) * tt + col % tt
            st = jnp.where(key <= tok, st, -jnp.inf)
        m_prev = m_sc[...]
        m_new = jnp.maximum(m_prev, jnp.max(st, axis=0, keepdims=True))
        corr = jnp.exp2(m_prev - m_new)
        p = jnp.exp2(st - m_new)
        l_sc[...] = l_sc[...] * corr + jnp.sum(p, axis=0, keepdims=True)
        acc_sc[...] = acc_sc[...] * corr + _dot(vt_ref[...], p.astype(BF16))
        m_sc[...] = m_new

    crosses_diagonal = (ki + 1) * tk - 1 > qi * tq

    @pl.when(crosses_diagonal)
    def _():
        update(True)

    @pl.when(jnp.logical_not(crosses_diagonal))
    def _():
        update(False)

    @pl.when((ki + 1) * tk >= (qi + 1) * tq)
    def _():
        for u in range(n_tiles):
            for hh in range(H_B):
                cs = slice((u * H_B + hh) * tt, (u * H_B + hh + 1) * tt)
                o_ref[u, hh] = jnp.transpose(acc_sc[:, cs] / l_sc[:, cs]).astype(o_ref.dtype)


def mla_prompt_attention(qcat, kcat, kvt, n_seq, seq):
    tt = qcat.shape[2]
    n_tiles = max(1, min(FLASH_TQ, seq) // tt)
    tq, tk = n_tiles * tt, min(FLASH_TK, seq)
    assert seq % tq == 0 and seq % tk == 0
    nq, nkb = seq // tq, seq // tk
    rows = tq * H_B
    qi_list, ki_list = [], []
    for qi in range(nq):
        for ki in range(-(-((qi + 1) * tq) // tk)):
            qi_list.append(qi)
            ki_list.append(ki)
    qi_tbl = jnp.asarray(np.array(qi_list, np.int32))
    ki_tbl = jnp.asarray(np.array(ki_list, np.int32))
    grid_spec = pltpu.PrefetchScalarGridSpec(
        num_scalar_prefetch=2,
        grid=(n_seq, len(qi_list)),
        in_specs=[pl.BlockSpec((n_tiles, H_B, tt, QCAT), lambda b, s, qt, kt: (b * nq + qt[s], 0, 0, 0)),
                  pl.BlockSpec((tk, QCAT), lambda b, s, qt, kt: (b * nkb + kt[s], 0)),
                  pl.BlockSpec((KV_LORA, tk), lambda b, s, qt, kt: (0, b * nkb + kt[s]))],
        out_specs=pl.BlockSpec((n_tiles, H_B, tt, KV_LORA), lambda b, s, qt, kt: (b * nq + qt[s], 0, 0, 0)),
        scratch_shapes=[pltpu.VMEM((1, rows), F32), pltpu.VMEM((1, rows), F32), pltpu.VMEM((KV_LORA, rows), F32)],
    )
    return pl.pallas_call(
        functools.partial(_flash_kernel, tq=tq, tk=tk),
        grid_spec=grid_spec,
        out_shape=jax.ShapeDtypeStruct(qcat.shape[:3] + (KV_LORA,), BF16),
        compiler_params=_params(("arbitrary", "arbitrary")),
        name="mla_prompt_attn",
    )(qi_tbl, ki_tbl, qcat, kcat, kvt)


PAGES_PER_CHUNK = 64
PAGED_STREAMS = 4


def _paged_kernel(pt_ref, q_ref, knew_ref, ckv_hbm, ckrt_hbm, o_ref, kv_buf, krt_buf, sems, *,
                  n_new, layer, page, chunk_pages, n_chunks, n_str):
    b = pl.program_id(0)
    n_b = pl.num_programs(0)
    rows = n_new * H_B

    def page_copies(seq, chunk, slot, j):
        phys = pt_ref[seq, chunk * chunk_pages + j]
        dst = pl.ds(pl.multiple_of(j * page, page), page)
        dst_t = pl.ds(pl.multiple_of(j * QK_ROPE, QK_ROPE), QK_ROPE)
        return (pltpu.make_async_copy(ckv_hbm.at[layer, phys], kv_buf.at[slot, dst], sems.at[0, slot]),
                pltpu.make_async_copy(ckrt_hbm.at[layer, phys], krt_buf.at[slot, dst_t], sems.at[1, slot]))

    def start_chunk(seq, chunk, slot):
        def body(j, carry):
            for cp in page_copies(seq, chunk, slot, j):
                cp.start()
            return carry
        lax.fori_loop(0, chunk_pages, body, 0)

    def wait_chunk(seq, chunk, slot):
        def body(j, carry):
            for cp in page_copies(seq, chunk, slot, j):
                cp.wait()
            return carry
        lax.fori_loop(0, chunk_pages, body, 0)

    @pl.when(b == 0)
    def _():
        start_chunk(0, 0, 0)

    q = q_ref[0]
    qf = q.astype(F32)
    kn = knew_ref[0].astype(F32)
    tok = lax.broadcasted_iota(jnp.int32, (rows, 1), 0) // H_B
    cols = []
    for j in range(n_new):
        sj = jnp.sum(qf * kn[j:j + 1, :], axis=-1, keepdims=True)
        cols.append(jnp.where(j <= tok, sj, -jnp.inf))
    m = cols[0]
    for j in range(1, n_new):
        m = jnp.maximum(m, cols[j])
    l = jnp.zeros((rows, 1), F32)
    acc = jnp.zeros((rows, KV_LORA), F32)
    for j in range(n_new):
        pj = jnp.exp2(cols[j] - m)
        l = l + pj
        acc = acc + pj * kn[j:j + 1, :KV_LORA]

    q_lat = q[:, :KV_LORA]
    q_rope = q[:, KV_LORA:KV_LORA + QK_ROPE]
    sp = chunk_pages // n_str
    state = [(m, l, acc)] + [(jnp.full((rows, 1), -jnp.inf, F32), jnp.zeros((rows, 1), F32),
                              jnp.zeros((rows, KV_LORA), F32)) for _ in range(n_str - 1)]
    for c in range(n_chunks):
        slot = c % 2
        if c + 1 < n_chunks:
            start_chunk(b, c + 1, 1 - slot)
        else:
            @pl.when(b + 1 < n_b)
            def _():
                start_chunk(b + 1, 0, 1 - slot)
        wait_chunk(b, c, slot)
        for st in range(n_str):
            m, l, acc = state[st]
            kv = kv_buf[slot, st * sp * page:(st + 1) * sp * page, :].astype(BF16)
            s_rope = [_dot(q_rope, krt_buf[slot, j * QK_ROPE:(j + 1) * QK_ROPE, :].astype(BF16))
                      for j in range(st * sp, (st + 1) * sp)]
            s = _dot(q_lat, kv, NT) + jnp.concatenate(s_rope, axis=-1)
            m_new = jnp.maximum(m, jnp.max(s, axis=-1, keepdims=True))
            corr = jnp.exp2(m - m_new)
            p = jnp.exp2(s - m_new)
            l = l * corr + jnp.sum(p, axis=-1, keepdims=True)
            acc = acc * corr + _dot(p.astype(BF16), kv)
            state[st] = (m_new, l, acc)
    m = state[0][0]
    for st in range(1, n_str):
        m = jnp.maximum(m, state[st][0])
    l = jnp.zeros((rows, 1), F32)
    acc = jnp.zeros((rows, KV_LORA), F32)
    for m_s, l_s, acc_s in state:
        w = jnp.exp2(m_s - m)
        l = l + l_s * w
        acc = acc + acc_s * w
    o_ref[0] = (acc / l).astype(o_ref.dtype)


def mla_sample_attention(qcat, kcat, cache_kv, cache_krt, page_table, layer, n_seq, n_new):
    n_pages = page_table.shape[1]
    page = cache_kv.shape[2]
    rows = n_new * H_B
    chunk_pages = min(PAGES_PER_CHUNK, n_pages // 2)
    assert n_pages % (2 * chunk_pages) == 0
    n_streams = math.gcd(PAGED_STREAMS, chunk_pages)
    n_chunks = n_pages // chunk_pages
    q3 = qcat.reshape(n_seq, rows, QCAT)
    k3 = kcat.reshape(n_seq, n_new, QCAT)
    grid_spec = pltpu.PrefetchScalarGridSpec(
        num_scalar_prefetch=1,
        grid=(n_seq,),
        in_specs=[pl.BlockSpec((1, rows, QCAT), lambda b, pt: (b, 0, 0)),
                  pl.BlockSpec((1, n_new, QCAT), lambda b, pt: (b, 0, 0)),
                  pl.BlockSpec(memory_space=pl.ANY),
                  pl.BlockSpec(memory_space=pl.ANY)],
        out_specs=pl.BlockSpec((1, rows, KV_LORA), lambda b, pt: (b, 0, 0)),
        scratch_shapes=[pltpu.VMEM((2, chunk_pages * page, KV_LORA), F32),
                        pltpu.VMEM((2, chunk_pages * QK_ROPE, page), F32),
                        pltpu.SemaphoreType.DMA((2, 2))],
    )
    o = pl.pallas_call(
        functools.partial(_paged_kernel, n_new=n_new, layer=layer, page=page, chunk_pages=chunk_pages,
                          n_chunks=n_chunks, n_str=n_streams),
        grid_spec=grid_spec,
        out_shape=jax.ShapeDtypeStruct((n_seq, rows, KV_LORA), BF16),
        compiler_params=_params(("arbitrary",)),
        name="mla_sample_attn",
    )(page_table, q3, k3, cache_kv, cache_krt)
    return o.reshape(n_seq * n_new, H_B * KV_LORA)


def _gla_cumsum_matrix(chunk):
    blocks = [np.tril(np.ones((chunk, chunk), np.float32))]
    m = chunk // 2
    while m >= 1:
        w = np.zeros((chunk, chunk), np.float32)
        for t in range(chunk):
            seg = (t // m) * m
            if (t % (2 * m)) >= m:
                w[t, seg:t + 1] = 1.0
            else:
                w[t, t + 1:seg + m] = 1.0
        blocks.append(w)
        m //= 2
    return np.concatenate(blocks, axis=0)


def _gla_kernel(q_ref, k_ref, v_ref, g_ref, wc_ref, s0_ref, o_ref, sfin_ref, s_sc, *, chunk):
    c_idx = pl.program_id(1)

    @pl.when(c_idx == 0)
    def _():
        s_sc[...] = s0_ref[0]

    n_lev = chunk.bit_length() - 1
    row = lax.broadcasted_iota(jnp.int32, (chunk, chunk), 0)
    col = lax.broadcasted_iota(jnp.int32, (chunk, chunk), 1)
    row_in_chunk = lax.broadcasted_iota(jnp.int32, (chunk, DK_A), 0)
    sums_all = _dot_hi_exact_lhs(wc_ref[...], g_ref[...])
    for hh in range(H_A):
        q = q_ref[:, hh * DK_A:(hh + 1) * DK_A]
        k = k_ref[:, hh * DK_A:(hh + 1) * DK_A]
        v = v_ref[:, hh * DV_A:(hh + 1) * DV_A].astype(BF16)
        sums = sums_all[:, hh * DK_A:(hh + 1) * DK_A]
        b = sums[:chunk]
        b_last = b[chunk - 1:chunk, :]
        s_prev = s_sc[hh]

        a = jnp.where(row == col, _dot(q.astype(BF16), k.astype(BF16), NT), 0.0)
        for lev in range(n_lev):
            m = chunk >> (lev + 1)
            scale = jnp.exp(sums[(lev + 1) * chunk:(lev + 2) * chunk])
            upper = (row_in_chunk % (2 * m)) >= m
            q_l = jnp.where(upper, q * scale, 0.0).astype(BF16)
            k_l = jnp.where(upper, 0.0, k * scale).astype(BF16)
            same_block = (row // (2 * m)) == (col // (2 * m))
            a = a + jnp.where(same_block, _dot(q_l, k_l, NT), 0.0)

        inter = _dot((q * jnp.exp(b)).astype(BF16), s_prev.astype(BF16))
        o_ref[:, hh * DV_A:(hh + 1) * DV_A] = inter + _dot(a.astype(BF16), v)

        k_dec = (k * jnp.exp(b_last - b)).astype(BF16)
        decay_col = jnp.transpose(jnp.broadcast_to(jnp.exp(b_last), (DK_A, DK_A)))[:, :1]
        s_sc[hh] = decay_col * s_prev + _dot(k_dec, v, TN)

    @pl.when(c_idx == pl.num_programs(1) - 1)
    def _():
        sfin_ref[0] = s_sc[...]


def gla(qg, kg, vg, logf, s0, n_seq, seq, chunk):
    assert seq % chunk == 0 and chunk & (chunk - 1) == 0 and chunk % 8 == 0
    n_chunks = seq // chunk
    wc = jnp.asarray(_gla_cumsum_matrix(chunk), BF16)
    nk = H_A * DK_A

    def tok(cols):
        return pl.BlockSpec((chunk, cols), lambda b, c: (b * n_chunks + c, 0))

    state_spec = pl.BlockSpec((1, H_A, DK_A, DV_A), lambda b, c: (b, 0, 0, 0))
    return pl.pallas_call(
        functools.partial(_gla_kernel, chunk=chunk),
        grid=(n_seq, n_chunks),
        in_specs=[tok(nk), tok(nk), tok(H_A * DV_A), tok(nk), _const_spec(wc.shape), state_spec],
        out_specs=[tok(H_A * DV_A), state_spec],
        out_shape=[jax.ShapeDtypeStruct((n_seq * seq, H_A * DV_A), F32),
                   jax.ShapeDtypeStruct((n_seq, H_A, DK_A, DV_A), F32)],
        scratch_shapes=[pltpu.VMEM((H_A, DK_A, DV_A), F32)],
        compiler_params=_params(("arbitrary", "arbitrary")),
        name="gla",
    )(qg, kg, vg, logf, wc, s0)


def _route(h2, wrt_ref, rbias_ref, tm):
    logits = _dot_hi3(wrt_ref[...], h2, NT)
    scores = _sigmoid(logits)
    sel = scores + rbias_ref[...]
    neg = -jnp.inf

    iota_g = lax.broadcasted_iota(jnp.int32, (GROUP_SIZE, tm), 0).astype(F32)
    g_rows = []
    for g in range(N_GROUPS):
        blk = sel[g * GROUP_SIZE:(g + 1) * GROUP_SIZE]
        m1 = jnp.max(blk, axis=0, keepdims=True)
        first = jnp.min(jnp.where(blk == m1, iota_g, float(GROUP_SIZE)), axis=0, keepdims=True)
        m2 = jnp.max(jnp.where(iota_g == first, neg, blk), axis=0, keepdims=True)
        g_rows.append(m1 + m2)
    g_score = jnp.concatenate(g_rows, axis=0)

    iota_n = lax.broadcasted_iota(jnp.int32, (N_GROUPS, tm), 0).astype(F32)
    g_keep = jnp.zeros((N_GROUPS, tm), F32)
    for _ in range(TOPK_GROUPS):
        mx = jnp.max(g_score, axis=0, keepdims=True)
        first = jnp.min(jnp.where(g_score == mx, iota_n, float(N_GROUPS)), axis=0, keepdims=True)
        hit = iota_n == first
        g_keep = jnp.where(hit, 1.0, g_keep)
        g_score = jnp.where(hit, neg, g_score)

    sel_m = jnp.concatenate(
        [jnp.where(g_keep[g:g + 1] > 0.0, sel[g * GROUP_SIZE:(g + 1) * GROUP_SIZE], neg) for g in range(N_GROUPS)], axis=0)

    iota_e = lax.broadcasted_iota(jnp.int32, (N_EXPERTS, tm), 0).astype(F32)
    ids, wts, hits = [], [], []
    for _ in range(TOP_K):
        mx = jnp.max(sel_m, axis=0, keepdims=True)
        first = jnp.min(jnp.where(sel_m == mx, iota_e, float(N_EXPERTS)), axis=0, keepdims=True)
        hit = iota_e == first
        ids.append(first)
        hits.append(hit)
        wts.append(jnp.sum(jnp.where(hit, scores, 0.0), axis=0, keepdims=True))
        sel_m = jnp.where(hit, neg, sel_m)
    ids = jnp.concatenate(ids, axis=0)
    wts = jnp.concatenate(wts, axis=0)
    wts = wts / jnp.sum(wts, axis=0, keepdims=True) * ROUTED_SCALE
    return ids, wts, hits


def _rank_in_expert(hits, cnt_sc, tm):
    chosen = jnp.where(hits[0], 1.0, 0.0)
    for hit in hits[1:]:
        chosen = jnp.where(hit, 1.0, chosen)
    earlier = (lax.broadcasted_iota(jnp.int32, (tm, tm), 0) < lax.broadcasted_iota(jnp.int32, (tm, tm), 1))
    prefix = _dot(chosen.astype(BF16), jnp.where(earlier, 1.0, 0.0).astype(BF16))
    base = cnt_sc[:, :1] + prefix
    ranks = [jnp.sum(jnp.where(hit, base, 0.0), axis=0, keepdims=True) for hit in hits]
    cnt_sc[...] = cnt_sc[...] + jnp.sum(chosen, axis=1, keepdims=True)
    return jnp.concatenate(ranks, axis=0)


def _merge_kernel(x_ref, scm_ref, shm_ref, gm_ref, scf_ref, shf_ref, olat_ref, ogla_ref,
                  w3_ref, wuv_ref, wb_ref, wa_ref, wo_ref, ggla_ref, ln1g_ref, ln1b_ref, wrt_ref, rbias_ref, cnt0_ref,
                  x1_ref, h2_ref, ids_ref, wts_ref, rank_ref, cnt_ref, cnt_sc, *, alpha):
    @pl.when(pl.program_id(0) == 0)
    def _():
        cnt_sc[...] = cnt0_ref[...]

    gb, rb, d = x_ref.shape
    tm = gb * rb
    x = x_ref[...]
    h = (x * (1.0 + scm_ref[...]) + shm_ref[...]).reshape(tm, d).astype(BF16)
    z3 = _dot(h, w3_ref[...])

    vb = [_dot(olat_ref[0, hh], wuv_ref[hh]).astype(BF16) for hh in range(H_B)]
    y_b = _dot(jnp.concatenate(vb, axis=-1), wb_ref[...])

    ga = []
    for hh in range(H_A):
        o_n = _rms(ogla_ref[:, hh * DV_A:(hh + 1) * DV_A], ggla_ref[...])
        ga.append((o_n * _silu(z3[:, hh * DV_A:(hh + 1) * DV_A])).astype(BF16))
    y_a = _dot(jnp.concatenate(ga, axis=-1), wa_ref[...])

    merged = _sigmoid(z3[:, d:2 * d]) * y_a + _sigmoid(z3[:, 2 * d:]) * y_b
    mix = _dot(merged.astype(BF16), wo_ref[...]).reshape(gb, rb, d)
    x1 = _layer_norm(alpha * x + gm_ref[...] * mix, ln1g_ref[...], ln1b_ref[...])
    x1_ref[...] = x1
    h2 = (x1 * (1.0 + scf_ref[...]) + shf_ref[...]).reshape(tm, d)
    h2_ref[...] = h2.astype(BF16)
    ids, wts, hits = _route(h2, wrt_ref, rbias_ref, tm)
    ids_ref[...] = ids.astype(jnp.int32)
    wts_ref[...] = wts
    rank_ref[...] = _rank_in_expert(hits, cnt_sc, tm).astype(jnp.int32)
    cnt_ref[...] = cnt_sc[...]


def merge_and_route(x, mods, olat, ogla, mw, cnt0, alpha):
    g, r, d = x.shape
    t = g * r
    x_spec, mod_spec, n_tiles = _group_specs(g, r, d)
    names = ["w3", "wuv", "wb", "wa", "wo", "ggla", "ln1g", "ln1b", "wrt", "rbias"]
    kt_spec = pl.BlockSpec((TOP_K, TOKEN_TILE), lambda i: (0, i))
    cnt_spec = _const_spec((N_EXPERTS, LANES))
    return pl.pallas_call(
        functools.partial(_merge_kernel, alpha=alpha),
        grid=(n_tiles,),
        in_specs=[x_spec] + [mod_spec] * 5
                 + [pl.BlockSpec((1, H_B, TOKEN_TILE, KV_LORA), lambda i: (i, 0, 0, 0)), _tok_spec(H_A * DV_A)]
                 + [_const_spec(mw[n].shape) for n in names] + [cnt_spec],
        out_specs=[x_spec, _tok_spec(d), kt_spec, kt_spec, kt_spec, cnt_spec],
        out_shape=[jax.ShapeDtypeStruct((g, r, d), F32), jax.ShapeDtypeStruct((t, d), BF16),
                   jax.ShapeDtypeStruct((TOP_K, t), jnp.int32), jax.ShapeDtypeStruct((TOP_K, t), F32),
                   jax.ShapeDtypeStruct((TOP_K, t), jnp.int32), jax.ShapeDtypeStruct((N_EXPERTS, LANES), F32)],
        scratch_shapes=[pltpu.VMEM((N_EXPERTS, LANES), F32)],
        compiler_params=_params(("arbitrary",)),
        name="merge_route",
    )(x, *mods, olat, ogla, *[mw[n] for n in names], cnt0)


def _moe_kernel(pb_ref, pe_ref, lo_ref, hi_ref, x_ref, wg_ref, wu_ref, wd_ref, y_ref, wg_sc, wu_sc, wd_sc):
    j = pl.program_id(0)
    prev = jnp.maximum(j - 1, 0)
    new_expert = jnp.logical_or(j == 0, pe_ref[j] != pe_ref[prev])
    first_visit = jnp.logical_or(j == 0, pb_ref[j] != pb_ref[prev])
    lo = lo_ref[j]
    hi = hi_ref[j]

    @pl.when(new_expert)
    def _():
        wg_sc[...] = wg_ref[0].astype(BF16)
        wu_sc[...] = wu_ref[0].astype(BF16)
        wd_sc[...] = wd_ref[0].astype(BF16)

    @pl.when(first_visit)
    def _():
        y_ref[...] = jnp.zeros(y_ref.shape, y_ref.dtype)

    @pl.when(hi > lo)
    def _():
        x = x_ref[...]
        gate = _dot(x, wg_sc[...])
        up = _dot(x, wu_sc[...])
        act = (_silu(gate) * up).astype(BF16)
        y = _dot(act, wd_sc[...])
        row = lax.broadcasted_iota(jnp.int32, (y.shape[0], 1), 0)
        mine = jnp.logical_and(row >= lo, row < hi)
        y_ref[...] = jnp.where(mine, y, y_ref[...].astype(F32)).astype(y_ref.dtype)


def routed_expert_blocks(x_sorted, pairs, w_gate, w_up, w_down):
    m, d = x_sorted.shape
    de = w_gate.shape[-1]
    n_pairs = pairs[0].shape[0]
    grid_spec = pltpu.PrefetchScalarGridSpec(
        num_scalar_prefetch=4,
        grid=(n_pairs,),
        in_specs=[pl.BlockSpec((MOE_ROWS, d), lambda j, pb, pe, lo, hi: (pb[j], 0)),
                  pl.BlockSpec((1, d, de), lambda j, pb, pe, lo, hi: (pe[j], 0, 0)),
                  pl.BlockSpec((1, d, de), lambda j, pb, pe, lo, hi: (pe[j], 0, 0)),
                  pl.BlockSpec((1, de, d), lambda j, pb, pe, lo, hi: (pe[j], 0, 0))],
        out_specs=pl.BlockSpec((MOE_ROWS, d), lambda j, pb, pe, lo, hi: (pb[j], 0)),
        scratch_shapes=[pltpu.VMEM((d, de), BF16), pltpu.VMEM((d, de), BF16), pltpu.VMEM((de, d), BF16)],
    )
    return pl.pallas_call(
        _moe_kernel,
        grid_spec=grid_spec,
        out_shape=jax.ShapeDtypeStruct((m, d), BF16),
        compiler_params=_params(("arbitrary",)),
        name="moe_experts",
    )(*pairs, x_sorted, w_gate, w_up, w_down)


def _pos_kernel(ids_ref, rank_ref, pstart_ref, pos_ref):
    k, tm = ids_ref.shape
    iota_e = lax.broadcasted_iota(jnp.int32, (N_EXPERTS, tm), 0)
    pstart = pstart_ref[:, :1]
    base = [jnp.sum(jnp.where(iota_e == ids_ref[j:j + 1, :], pstart, 0.0), axis=0, keepdims=True) for j in range(k)]
    pos_ref[...] = jnp.concatenate(base, axis=0).astype(jnp.int32) + rank_ref[...]


def dispatch(ids, rank, counts):
    k, t = ids.shape
    m = k * t
    assert m % MOE_ROWS == 0
    n_blk = m // MOE_ROWS
    counts = counts[:, 0].astype(jnp.int32)
    uend = jnp.cumsum(counts)
    ustart = uend - counts
    pstart = jnp.broadcast_to(ustart.astype(F32)[:, None], (N_EXPERTS, LANES))
    kt_spec = pl.BlockSpec((k, TOKEN_TILE), lambda i: (0, i))
    pos = pl.pallas_call(
        _pos_kernel,
        grid=(t // TOKEN_TILE,),
        in_specs=[kt_spec, kt_spec, _const_spec((N_EXPERTS, LANES))],
        out_specs=kt_spec,
        out_shape=jax.ShapeDtypeStruct((k, t), jnp.int32),
        compiler_params=_params(("arbitrary",)),
        name="dispatch_positions",
    )(ids, rank, pstart)
    tok = jnp.broadcast_to(jnp.arange(t, dtype=jnp.int32)[None, :], (k, t))
    _, tok_sorted = lax.sort_key_val(pos.reshape(m), tok.reshape(m))

    first_blk = ustart // MOE_ROWS
    n_pairs_e = jnp.where(counts > 0, (uend - 1) // MOE_ROWS - first_blk + 1, 0)
    pair_end = jnp.cumsum(n_pairs_e)
    pair_start = pair_end - n_pairs_e
    total = pair_end[-1]
    j = jnp.arange(n_blk + N_EXPERTS, dtype=jnp.int32)
    valid = j < total
    e_j = jnp.minimum(jnp.searchsorted(pair_end, j, side="right"), N_EXPERTS - 1).astype(jnp.int32)
    e_j = jnp.where(valid, e_j, e_j[total - 1])
    b_j = jnp.where(valid, first_blk[e_j] + j - pair_start[e_j], n_blk - 1).astype(jnp.int32)
    lo = jnp.where(valid, jnp.maximum(ustart[e_j], b_j * MOE_ROWS) - b_j * MOE_ROWS, 0).astype(jnp.int32)
    hi = jnp.where(valid, jnp.minimum(uend[e_j], (b_j + 1) * MOE_ROWS) - b_j * MOE_ROWS, 0).astype(jnp.int32)
    return tok_sorted, pos, (b_j, e_j, lo, hi)


def _final_kernel(x1_ref, gf_ref, h2_ref, yg_ref, wts_ref, wsg_ref, wsu_ref, wsd_ref, ln2g_ref, ln2b_ref, y_ref, *, alpha):
    gb, rb, d = x1_ref.shape
    h2 = h2_ref[...]
    act = (_silu(_dot(h2, wsg_ref[...])) * _dot(h2, wsu_ref[...])).astype(BF16)
    ffn = _dot(act, wsd_ref[...])
    w = wts_ref[...]
    routed = yg_ref[0].astype(F32) * w[:, 0:1]
    for j in range(1, TOP_K):
        routed = routed + yg_ref[j].astype(F32) * w[:, j:j + 1]
    ffn = (ffn + routed).reshape(gb, rb, d)
    y_ref[...] = _layer_norm(alpha * x1_ref[...] + gf_ref[...] * ffn, ln2g_ref[...], ln2b_ref[...])


def shared_combine_norm(x1, gf, h2, yg, wts_t, fw, alpha):
    g, r, d = x1.shape
    x_spec, mod_spec, n_tiles = _group_specs(g, r, d)
    names = ["wsg", "wsu", "wsd", "ln2g", "ln2b"]
    return pl.pallas_call(
        functools.partial(_final_kernel, alpha=alpha),
        grid=(n_tiles,),
        in_specs=[x_spec, mod_spec, _tok_spec(d), pl.BlockSpec((TOP_K, TOKEN_TILE, d), lambda i: (0, i, 0)),
                  _tok_spec(TOP_K)] + [_const_spec(fw[n].shape) for n in names],
        out_specs=x_spec,
        out_shape=jax.ShapeDtypeStruct((g, r, d), F32),
        compiler_params=_params(("arbitrary",)),
        name="shared_combine_norm",
    )(x1, gf, h2, yg, wts_t, *[fw[n] for n in names])


def _rot_cols(w):
    half = w.shape[-1] // 2
    return jnp.concatenate([-w[..., half:], w[..., :half]], axis=-1)


def _pad_cols(w, width):
    return jnp.pad(w, [(0, 0)] * (w.ndim - 1) + [(0, width - w.shape[-1])])


def _prepare_weights(w):
    cuts = np.cumsum((0,) + IN_SIZES)
    part = [w["w_in"][:, cuts[i]:cuts[i + 1]] for i in range(len(IN_SIZES))]
    w_qa, w_kva, w_kr, w_gq, w_gk, w_gv, w_go, w_gf, w_ga, w_gb = part
    w1 = jnp.concatenate([w_qa, w_kva, _pad_cols(w_kr, LANES), _pad_cols(_rot_cols(w_kr), LANES)], axis=1)
    wuq = w["w_uq"].reshape(Q_LORA, H_B, QK_NOPE + QK_ROPE)
    wuq_rope = wuq[:, :, QK_NOPE:]
    wq = jnp.concatenate([wuq[:, :, :QK_NOPE].reshape(Q_LORA, H_B * QK_NOPE),
                          _pad_cols(wuq_rope, LANES).reshape(Q_LORA, H_B * LANES),
                          _pad_cols(_rot_cols(wuq_rope), LANES).reshape(Q_LORA, H_B * LANES)], axis=1)
    wg = jnp.concatenate([w_gq, w_gk, w_gv, _pad_cols(w_gf, LANES)], axis=1)
    proj = dict(
        w1=w1.astype(BF16), wq=wq.astype(BF16), wg=wg.astype(BF16),
        wuk=jnp.transpose(w["w_uk"], (1, 2, 0)).astype(BF16),
        wf2=jnp.pad(w["w_gla_f2"], ((0, LANES - GLA_LR), (0, 0))).astype(BF16),
        bf=w["b_gla_f"].reshape(1, -1), gq=w["g_q_norm"].reshape(1, -1), gkv=w["g_kv_norm"].reshape(1, -1))
    merge = dict(
        w3=jnp.concatenate([w_go, w_ga, w_gb], axis=1).astype(BF16),
        wuv=jnp.transpose(w["w_uv"], (1, 0, 2)).astype(BF16),
        wb=w["w_b_out"].astype(BF16), wa=w["w_a_out"].astype(BF16), wo=w["w_o"].astype(BF16),
        ggla=w["g_gla_norm"].reshape(1, -1), ln1g=w["ln1_g"].reshape(1, -1), ln1b=w["ln1_b"].reshape(1, -1),
        wrt=jnp.transpose(w["w_router"]), rbias=w["router_bias"].reshape(-1, 1))
    final = dict(
        wsg=w["w_s_gate"].astype(BF16), wsu=w["w_s_up"].astype(BF16), wsd=w["w_s_down"].astype(BF16),
        ln2g=w["ln2_g"].reshape(1, -1), ln2b=w["ln2_b"].reshape(1, -1))
    return proj, merge, final


def _rope_tables(pos, tile_rows):
    half = QK_ROPE // 2
    freqs = ROPE_THETA ** (-jnp.arange(half, dtype=F32) / half)
    ang = pos.astype(F32)[:, None] * freqs
    cos = _pad_cols(jnp.concatenate([jnp.cos(ang)] * 2, axis=-1), LANES)
    sin = _pad_cols(jnp.concatenate([jnp.sin(ang)] * 2, axis=-1), LANES)
    if pos.shape[0] < tile_rows:
        rep = tile_rows // pos.shape[0]
        cos, sin = jnp.tile(cos, (rep, 1)), jnp.tile(sin, (rep, 1))
    return cos, sin


def _split_mods(mod, n):
    return [m.reshape(n, 1, D_MODEL) for m in jnp.split(mod, 6, axis=-1)]


def kernel(x_prompt, x_sample, c_prompt, c_sample, cache_kv_latent, cache_k_rope, state_gla, page_table, w_ada, b_ada, w_in, g_q_norm, w_uq, g_kv_norm, w_uk, w_uv, w_gla_f2, b_gla_f, g_gla_norm, w_a_out, w_b_out, w_o, ln1_g, ln1_b, w_router, router_bias, w_e_gate, w_e_up, w_e_down, w_s_gate, w_s_up, w_s_down, ln2_g, ln2_b):
    w_all = dict(w_ada=w_ada, b_ada=b_ada, w_in=w_in, g_q_norm=g_q_norm, w_uq=w_uq, g_kv_norm=g_kv_norm, w_uk=w_uk,
                 w_uv=w_uv, w_gla_f2=w_gla_f2, b_gla_f=b_gla_f, g_gla_norm=g_gla_norm, w_a_out=w_a_out,
                 w_b_out=w_b_out, w_o=w_o, ln1_g=ln1_g, ln1_b=ln1_b, w_router=w_router, router_bias=router_bias,
                 w_e_gate=w_e_gate, w_e_up=w_e_up, w_e_down=w_e_down, w_s_gate=w_s_gate, w_s_up=w_s_up,
                 w_s_down=w_s_down, ln2_g=ln2_g, ln2_b=ln2_b)
    depth = w_ada.shape[0]
    alpha = (2.0 * depth) ** 0.25
    n_p, s_p, _ = x_prompt.shape
    n_s, s_s, _ = x_sample.shape
    t_p, t_s = n_p * s_p, n_s * s_s
    past_len = page_table.shape[1] * cache_kv_latent.shape[2]
    cache_krt = jnp.swapaxes(cache_k_rope, 2, 3)
    cos_p, sin_p = _rope_tables(jnp.arange(s_p, dtype=jnp.int32), TOKEN_TILE)
    cos_s, sin_s = _rope_tables(past_len + jnp.arange(s_s, dtype=jnp.int32), TOKEN_TILE)

    n_c = n_p + n_s
    n_c_pad = -(-n_c // 16) * 16
    c_all = jnp.pad(jnp.concatenate([c_prompt, c_sample], axis=0), ((0, n_c_pad - n_c), (0, 0)))

    yp, ys = x_prompt, x_sample
    outs = [[] for _ in range(6)]
    for layer in range(depth):
        w = {name: arr[layer] for name, arr in w_all.items()}
        pw, mw, fw = _prepare_weights(w)
        mod = adaln(c_all, w["w_ada"], w["b_ada"])
        mods_p = _split_mods(mod[:n_p], n_p)
        mods_s = _split_mods(mod[n_p:n_c], n_s)

        ckv_p, kr_p, kcat_p, kvt_p, qcat_p, qg_p, kg_p, vg_p, lf_p = input_projections(
            yp, mods_p[1], mods_p[0], cos_p, sin_p, pw)
        olat_p = mla_prompt_attention(qcat_p, kcat_p, kvt_p, n_p, s_p)
        s0_p = jnp.zeros((n_p, H_A, DK_A, DV_A), F32)
        ogla_p, sfin_p = gla(qg_p, kg_p, vg_p, lf_p, s0_p, n_p, s_p, min(GLA_CHUNK_PROMPT, s_p))
        cnt0 = jnp.zeros((N_EXPERTS, LANES), F32)
        x1_p, h2_p, ids_p, wts_p, rank_p, cnt_p = merge_and_route(
            yp, [mods_p[1], mods_p[0], mods_p[2], mods_p[4], mods_p[3]], olat_p, ogla_p, mw, cnt0, alpha)

        ckv_s, kr_s, kcat_s, _, qcat_s, qg_s, kg_s, vg_s, lf_s = input_projections(
            ys, mods_s[1], mods_s[0], cos_s, sin_s, pw)
        qtok_s = jnp.transpose(qcat_s, (0, 2, 1, 3)).reshape(t_s, H_B * QCAT)
        otok_s = mla_sample_attention(qtok_s, kcat_s, cache_kv_latent, cache_krt, page_table, layer, n_s, s_s)
        olat_s = jnp.transpose(otok_s.reshape(t_s // TOKEN_TILE, TOKEN_TILE, H_B, KV_LORA), (0, 2, 1, 3))
        ogla_s, sfin_s = gla(qg_s, kg_s, vg_s, lf_s, state_gla[layer], n_s, s_s, s_s)
        x1_s, h2_s, ids_s, wts_s, rank_s, cnt = merge_and_route(
            ys, [mods_s[1], mods_s[0], mods_s[2], mods_s[4], mods_s[3]], olat_s, ogla_s, mw, cnt_p, alpha)

        h2 = jnp.concatenate([h2_p, h2_s], axis=0)
        ids = jnp.concatenate([ids_p, ids_s], axis=1)
        rank = jnp.concatenate([rank_p, rank_s], axis=1)
        tok_sorted, pos, pairs = dispatch(ids, rank, cnt)
        x_sorted = h2.at[tok_sorted].get(mode="promise_in_bounds")
        y_sorted = routed_expert_blocks(x_sorted, pairs, w["w_e_gate"], w["w_e_up"], w["w_e_down"])
        yg_p = y_sorted.at[pos[:, :t_p]].get(mode="promise_in_bounds")
        yg_s = y_sorted.at[pos[:, t_p:]].get(mode="promise_in_bounds")

        yp = shared_combine_norm(x1_p, mods_p[5], h2_p, yg_p, jnp.transpose(wts_p), fw, alpha)
        ys = shared_combine_norm(x1_s, mods_s[5], h2_s, yg_s, jnp.transpose(wts_s), fw, alpha)

        for lst, val in zip(outs, [ckv_p.reshape(n_p, s_p, KV_LORA), kr_p.reshape(n_p, s_p, QK_ROPE), sfin_p,
                                   ckv_s.reshape(n_s, s_s, KV_LORA), kr_s.reshape(n_s, s_s, QK_ROPE), sfin_s]):
            lst.append(val.astype(state_gla.dtype) if val.ndim == 4 else val)
    return (yp, ys) + tuple(jnp.stack(o) for o in outs)
```

```python
import functools
import math

import numpy as np
import jax
import jax.numpy as jnp
from jax import lax
from jax.experimental import pallas as pl
from jax.experimental.pallas import tpu as pltpu

F32 = jnp.float32
BF16 = jnp.bfloat16

D_MODEL = 1024
H_A, DK_A, DV_A, GLA_LR, GLA_TAU = 4, 128, 256, 16, 16.0
H_B, Q_LORA, KV_LORA, QK_NOPE, QK_ROPE, V_HEAD = 8, 384, 256, 128, 64, 128
ROPE_THETA = 10000.0
ATTN_SCALE = (QK_NOPE + QK_ROPE) ** -0.5
Q_SCALE = ATTN_SCALE * 1.4426950408889634
N_EXPERTS, TOP_K, N_GROUPS, TOPK_GROUPS = 256, 8, 8, 4
GROUP_SIZE = N_EXPERTS // N_GROUPS
D_EXPERT, D_SHARED, ROUTED_SCALE = 256, 256, 2.5
EPS = 1e-6
IN_SIZES = (Q_LORA, KV_LORA, QK_ROPE, H_A * DK_A, H_A * DK_A, H_A * DV_A, H_A * DV_A, GLA_LR, D_MODEL, D_MODEL)

LANES = 128
QCAT = KV_LORA + LANES
VMEM_LIMIT = 56 * 1024 * 1024

TOKEN_TILE = 512
FLASH_TQ, FLASH_TK = 512, 512
GLA_CHUNK_PROMPT = 128
MOE_ROWS = 512

NN = (((1,), (0,)), ((), ()))
NT = (((1,), (1,)), ((), ()))
TN = (((0,), (0,)), ((), ()))


def _dot(a, b, dims=NN):
    return lax.dot_general(a, b, dims, preferred_element_type=F32)


def _split3(x):
    x1 = x.astype(BF16)
    r1 = x - x1.astype(F32)
    x2 = r1.astype(BF16)
    x3 = (r1 - x2.astype(F32)).astype(BF16)
    return x1, x2, x3


def _dot_hi(a, b, dims=NN):
    a1, a2, a3 = _split3(a)
    b1, b2, b3 = _split3(b)
    small = _dot(a3, b1, dims) + _dot(a2, b2, dims) + _dot(a1, b3, dims)
    mid = _dot(a2, b1, dims) + _dot(a1, b2, dims)
    return (small + mid) + _dot(a1, b1, dims)


def _dot_hi3(a, b, dims=NN):
    a1 = a.astype(BF16)
    a2 = (a - a1.astype(F32)).astype(BF16)
    b1 = b.astype(BF16)
    b2 = (b - b1.astype(F32)).astype(BF16)
    return (_dot(a2, b1, dims) + _dot(a1, b2, dims)) + _dot(a1, b1, dims)


def _dot_hi_exact_lhs(w, x, dims=NN):
    x1, x2, x3 = _split3(x)
    return (_dot(w, x3, dims) + _dot(w, x2, dims)) + _dot(w, x1, dims)


def _sigmoid(x):
    return 1.0 / (1.0 + jnp.exp(-x))


def _silu(x):
    return x * _sigmoid(x)


def _rms(x, g):
    return x * lax.rsqrt(jnp.mean(x * x, axis=-1, keepdims=True) + EPS) * g


def _layer_norm(x, g, b):
    mu = jnp.mean(x, axis=-1, keepdims=True)
    xc = x - mu
    var = jnp.mean(xc * xc, axis=-1, keepdims=True)
    return xc * lax.rsqrt(var + EPS) * g + b


def _params(sem):
    return pltpu.CompilerParams(dimension_semantics=sem, vmem_limit_bytes=VMEM_LIMIT)


def _const_spec(shape):
    nd = len(shape)
    return pl.BlockSpec(shape, lambda *_: (0,) * nd)


def _adaln_kernel(c_ref, w_ref, b_ref, o_ref):
    o_ref[...] = _dot_hi(_silu(c_ref[...]), w_ref[...]) + b_ref[...]


def adaln(c, w_ada, b_ada):
    n, d = c.shape
    e = w_ada.shape[1]
    tn = 512
    return pl.pallas_call(
        _adaln_kernel,
        grid=(e // tn,),
        in_specs=[_const_spec((n, d)), pl.BlockSpec((d, tn), lambda j: (0, j)), pl.BlockSpec((1, tn), lambda j: (0, j))],
        out_specs=pl.BlockSpec((n, tn), lambda j: (0, j)),
        out_shape=jax.ShapeDtypeStruct((n, e), F32),
        compiler_params=_params(("arbitrary",)),
        name="adaln",
    )(c, w_ada, b_ada.reshape(1, e))


W1_COLS = Q_LORA + KV_LORA + 2 * LANES
WQ_COLS = H_B * QK_NOPE + 2 * H_B * LANES
WG_COLS = 2 * H_A * DK_A + H_A * DV_A + LANES


def _proj_kernel(x_ref, sc_ref, sh_ref, cos_ref, sin_ref, w1_ref, wq_ref, wuk_ref, wg_ref, wf2_ref, bf_ref,
                 gq_ref, gkv_ref,
                 ckv_ref, krope_ref, kcat_ref, kvt_ref, qcat_ref, qg_ref, kg_ref, vg_ref, logf_ref):
    gb, rb, d = x_ref.shape
    tm = gb * rb
    h = (x_ref[...] * (1.0 + sc_ref[...]) + sh_ref[...]).reshape(tm, d).astype(BF16)
    cos = cos_ref[...]
    sin = sin_ref[...]

    z1 = _dot(h, w1_ref[...])
    ckv = _rms(z1[:, Q_LORA:Q_LORA + KV_LORA], gkv_ref[...])
    o_kr = Q_LORA + KV_LORA
    krope = z1[:, o_kr:o_kr + LANES] * cos + z1[:, o_kr + LANES:o_kr + 2 * LANES] * sin
    ckv_ref[...] = ckv
    krope_ref[...] = krope[:, :QK_ROPE]
    kcat_ref[:, :KV_LORA] = ckv.astype(BF16)
    kcat_ref[:, KV_LORA:] = krope.astype(BF16)
    kvt_ref[...] = jnp.transpose(ckv).astype(BF16)

    qn = _rms(z1[:, :Q_LORA], gq_ref[...]).astype(BF16)
    q2 = _dot(qn, wq_ref[...])
    o_r = H_B * QK_NOPE
    o_s = o_r + H_B * LANES
    for hh in range(H_B):
        q_nope = q2[:, hh * QK_NOPE:(hh + 1) * QK_NOPE].astype(BF16)
        q_lat = _dot(q_nope, wuk_ref[hh])
        q_rope = q2[:, o_r + hh * LANES:o_r + (hh + 1) * LANES] * cos + q2[:, o_s + hh * LANES:o_s + (hh + 1) * LANES] * sin
        qcat_ref[0, hh, :, :KV_LORA] = (q_lat * Q_SCALE).astype(BF16)
        qcat_ref[0, hh, :, KV_LORA:] = (q_rope * Q_SCALE).astype(BF16)

    z2 = _dot(h, wg_ref[...])
    nk = H_A * DK_A
    qg_ref[...] = z2[:, :nk] * DK_A ** -0.5
    kg_ref[...] = z2[:, nk:2 * nk]
    vg_ref[...] = z2[:, 2 * nk:2 * nk + H_A * DV_A]
    fa = z2[:, 2 * nk + H_A * DV_A:].astype(BF16)
    f_pre = _dot(fa, wf2_ref[...]) + bf_ref[...]
    log_sig = jnp.minimum(f_pre, 0.0) - jnp.log(1.0 + jnp.exp(-jnp.abs(f_pre)))
    logf_ref[...] = log_sig / GLA_TAU


def _group_tiling(g, r):
    if r >= TOKEN_TILE:
        assert r % TOKEN_TILE == 0
        return 1, TOKEN_TILE
    assert TOKEN_TILE % r == 0 and g % (TOKEN_TILE // r) == 0 and r % 8 == 0
    return TOKEN_TILE // r, r


def _group_specs(g, r, d):
    gb, rb = _group_tiling(g, r)
    nr = r // rb
    x_spec = pl.BlockSpec((gb, rb, d), lambda i: (i // nr, i % nr, 0))
    mod_spec = pl.BlockSpec((gb, 1, d), lambda i: (i // nr, 0, 0))
    return x_spec, mod_spec, (g // gb) * nr


def _tok_spec(cols):
    return pl.BlockSpec((TOKEN_TILE, cols), lambda i: (i, 0))


def input_projections(x, sc, sh, cos_tbl, sin_tbl, pw):
    g, r, d = x.shape
    t = g * r
    x_spec, mod_spec, n_tiles = _group_specs(g, r, d)
    n_tbl = cos_tbl.shape[0] // TOKEN_TILE
    tbl_spec = pl.BlockSpec((TOKEN_TILE, LANES), lambda i: (i % n_tbl, 0))
    nk = H_A * DK_A
    head_cols = [(KV_LORA, F32), (QK_ROPE, F32), (QCAT, BF16)]
    tail_cols = [(nk, F32), (nk, F32), (H_A * DV_A, F32), (nk, F32)]
    mid_specs = [pl.BlockSpec((KV_LORA, TOKEN_TILE), lambda i: (0, i)),
                 pl.BlockSpec((1, H_B, TOKEN_TILE, QCAT), lambda i: (i, 0, 0, 0))]
    mid_shapes = [jax.ShapeDtypeStruct((KV_LORA, t), BF16), jax.ShapeDtypeStruct((n_tiles, H_B, TOKEN_TILE, QCAT), BF16)]
    return pl.pallas_call(
        _proj_kernel,
        grid=(n_tiles,),
        in_specs=[x_spec, mod_spec, mod_spec, tbl_spec, tbl_spec,
                  _const_spec(pw["w1"].shape), _const_spec(pw["wq"].shape), _const_spec(pw["wuk"].shape),
                  _const_spec(pw["wg"].shape), _const_spec(pw["wf2"].shape), _const_spec(pw["bf"].shape),
                  _const_spec(pw["gq"].shape), _const_spec(pw["gkv"].shape)],
        out_specs=[_tok_spec(c) for c, _ in head_cols] + mid_specs + [_tok_spec(c) for c, _ in tail_cols],
        out_shape=[jax.ShapeDtypeStruct((t, c), dt) for c, dt in head_cols] + mid_shapes
                  + [jax.ShapeDtypeStruct((t, c), dt) for c, dt in tail_cols],
        compiler_params=_params(("arbitrary",)),
        name="input_proj",
    )(x, sc, sh, cos_tbl, sin_tbl, pw["w1"], pw["wq"], pw["wuk"], pw["wg"], pw["wf2"], pw["bf"], pw["gq"], pw["gkv"])


def _flash_kernel(qi_ref, ki_ref, q_ref, k_ref, vt_ref, o_ref, m_sc, l_sc, acc_sc, *, tq, tk):
    step = pl.program_id(1)
    qi = qi_ref[step]
    ki = ki_ref[step]
    n_tiles, _, tt, _ = q_ref.shape
    cols = n_tiles * H_B * tt

    @pl.when(ki == 0)
    def _():
        m_sc[...] = jnp.full(m_sc.shape, -jnp.inf, F32)
        l_sc[...] = jnp.zeros(l_sc.shape, F32)
        acc_sc[...] = jnp.zeros(acc_sc.shape, F32)

    def update(masked):
        st = _dot(k_ref[...], q_ref[...].reshape(cols, QCAT), NT)
        if masked:
            key = ki * tk + lax.broadcasted_iota(jnp.int32, (tk, cols), 0)
            col = lax.broadcasted_iota(jnp.int32, (tk, cols), 1)
            tok = qi * tq + (col // (H_B * tt)) * tt + col % tt
            st = jnp.where(key <= tok, st, -jnp.inf)
        m_prev = m_sc[...]
        m_new = jnp.maximum(m_prev, jnp.max(st, axis=0, keepdims=True))
        corr = jnp.exp2(m_prev - m_new)
        p = jnp.exp2(st - m_new)
        l_sc[...] = l_sc[...] * corr + jnp.sum(p, axis=0, keepdims=True)
        acc_sc[...] = acc_sc[...] * corr + _dot(vt_ref[...], p.astype(BF16))
        m_sc[...] = m_new

    crosses_diagonal = (ki + 1) * tk - 1 > qi * tq

    @pl.when(crosses_diagonal)
    def _():
        update(True)

    @pl.when(jnp.logical_not(crosses_diagonal))
    def _():
        update(False)

    @pl.when((ki + 1) * tk >= (qi + 1) * tq)
    def _():
        for u in range(n_tiles):
            for hh in range(H_B):
                cs = slice((u * H_B + hh) * tt, (u * H_B + hh + 1) * tt)
                o_ref[u, hh] = jnp.transpose(acc_sc[:, cs] / l_sc[:, cs]).astype(o_ref.dtype)


def mla_prompt_attention(qcat, kcat, kvt, n_seq, seq):
    tt = qcat.shape[2]
    n_tiles = max(1, min(FLASH_TQ, seq) // tt)
    tq, tk = n_tiles * tt, min(FLASH_TK, seq)
    assert seq % tq == 0 and seq % tk == 0
    nq, nkb = seq // tq, seq // tk
    rows = tq * H_B
    qi_list, ki_list = [], []
    for qi in range(nq):
        for ki in range(-(-((qi + 1) * tq) // tk)):
            qi_list.append(qi)
            ki_list.append(ki)
    qi_tbl = jnp.asarray(np.array(qi_list, np.int32))
    ki_tbl = jnp.asarray(np.array(ki_list, np.int32))
    grid_spec = pltpu.PrefetchScalarGridSpec(
        num_scalar_prefetch=2,
        grid=(n_seq, len(qi_list)),
        in_specs=[pl.BlockSpec((n_tiles, H_B, tt, QCAT), lambda b, s, qt, kt: (b * nq + qt[s], 0, 0, 0)),
                  pl.BlockSpec((tk, QCAT), lambda b, s, qt, kt: (b * nkb + kt[s], 0)),
                  pl.BlockSpec((KV_LORA, tk), lambda b, s, qt, kt: (0, b * nkb + kt[s]))],
        out_specs=pl.BlockSpec((n_tiles, H_B, tt, KV_LORA), lambda b, s, qt, kt: (b * nq + qt[s], 0, 0, 0)),
        scratch_shapes=[pltpu.VMEM((1, rows), F32), pltpu.VMEM((1, rows), F32), pltpu.VMEM((KV_LORA, rows), F32)],
    )
    return pl.pallas_call(
        functools.partial(_flash_kernel, tq=tq, tk=tk),
        grid_spec=grid_spec,
        out_shape=jax.ShapeDtypeStruct(qcat.shape[:3] + (KV_LORA,), BF16),
        compiler_params=_params(("arbitrary", "arbitrary")),
        name="mla_prompt_attn",
    )(qi_tbl, ki_tbl, qcat, kcat, kvt)


PAGES_PER_CHUNK = 64
PAGED_STREAMS = 4


def _paged_kernel(pt_ref, q_ref, knew_ref, ckv_hbm, ckrt_hbm, o_ref, kv_buf, krt_buf, sems, *,
                  n_new, layer, page, chunk_pages, n_chunks, n_str):
    b = pl.program_id(0)
    n_b = pl.num_programs(0)
    rows = n_new * H_B

    def page_copies(seq, chunk, slot, j):
        phys = pt_ref[seq, chunk * chunk_pages + j]
        dst = pl.ds(j * page, page)
        dst_t = pl.ds(j * QK_ROPE, QK_ROPE)
        return (pltpu.make_async_copy(ckv_hbm.at[layer, phys], kv_buf.at[slot, dst], sems.at[0, slot]),
                pltpu.make_async_copy(ckrt_hbm.at[layer, phys], krt_buf.at[slot, dst_t], sems.at[1, slot]))

    def start_chunk(seq, chunk, slot):
        for j in range(chunk_pages):
            for cp in page_copies(seq, chunk, slot, j):
                cp.start()

    def wait_chunk(seq, chunk, slot):
        for j in range(chunk_pages):
            for cp in page_copies(seq, chunk, slot, j):
                cp.wait()

    @pl.when(b == 0)
    def _():
        start_chunk(0, 0, 0)

    q = q_ref[0]
    qf = q.astype(F32)
    kn = knew_ref[0].astype(F32)
    tok = lax.broadcasted_iota(jnp.int32, (rows, 1), 0) // H_B
    cols = []
    for j in range(n_new):
        sj = jnp.sum(qf * kn[j:j + 1, :], axis=-1, keepdims=True)
        cols.append(jnp.where(j <= tok, sj, -jnp.inf))
    m = cols[0]
    for j in range(1, n_new):
        m = jnp.maximum(m, cols[j])
    l = jnp.zeros((rows, 1), F32)
    acc = jnp.zeros((rows, KV_LORA), F32)
    for j in range(n_new):
        pj = jnp.exp2(cols[j] - m)
        l = l + pj
        acc = acc + pj * kn[j:j + 1, :KV_LORA]

    q_lat = q[:, :KV_LORA]
    q_rope = q[:, KV_LORA:KV_LORA + QK_ROPE]
    sp = chunk_pages // n_str
    state = [(m, l, acc)] + [(jnp.full((rows, 1), -jnp.inf, F32), jnp.zeros((rows, 1), F32),
                              jnp.zeros((rows, KV_LORA), F32)) for _ in range(n_str - 1)]
    for c in range(n_chunks):
        slot = c % 2
        if c + 1 < n_chunks:
            start_chunk(b, c + 1, 1 - slot)
        else:
            @pl.when(b + 1 < n_b)
            def _():
                start_chunk(b + 1, 0, 1 - slot)
        wait_chunk(b, c, slot)
        for st in range(n_str):
            m, l, acc = state[st]
            kv = kv_buf[slot, st * sp * page:(st + 1) * sp * page, :].astype(BF16)
            s_rope = [_dot(q_rope, krt_buf[slot, j * QK_ROPE:(j + 1) * QK_ROPE, :].astype(BF16))
                      for j in range(st * sp, (st + 1) * sp)]
            s = _dot(q_lat, kv, NT) + jnp.concatenate(s_rope, axis=-1)
            m_new = jnp.maximum(m, jnp.max(s, axis=-1, keepdims=True))
            corr = jnp.exp2(m - m_new)
            p = jnp.exp2(s - m_new)
            l = l * corr + jnp.sum(p, axis=-1, keepdims=True)
            acc = acc * corr + _dot(p.astype(BF16), kv)
            state[st] = (m_new, l, acc)
    m = state[0][0]
    for st in range(1, n_str):
        m = jnp.maximum(m, state[st][0])
    l = jnp.zeros((rows, 1), F32)
    acc = jnp.zeros((rows, KV_LORA), F32)
    for m_s, l_s, acc_s in state:
        w = jnp.exp2(m_s - m)
        l = l + l_s * w
        acc = acc + acc_s * w
    o_ref[0] = (acc / l).astype(o_ref.dtype)


def mla_sample_attention(qcat, kcat, cache_kv, cache_krt, page_table, layer, n_seq, n_new):
    n_pages = page_table.shape[1]
    page = cache_kv.shape[2]
    rows = n_new * H_B
    chunk_pages = min(PAGES_PER_CHUNK, n_pages // 2)
    assert n_pages % (2 * chunk_pages) == 0
    n_streams = math.gcd(PAGED_STREAMS, chunk_pages)
    n_chunks = n_pages // chunk_pages
    q3 = qcat.reshape(n_seq, rows, QCAT)
    k3 = kcat.reshape(n_seq, n_new, QCAT)
    grid_spec = pltpu.PrefetchScalarGridSpec(
        num_scalar_prefetch=1,
        grid=(n_seq,),
        in_specs=[pl.BlockSpec((1, rows, QCAT), lambda b, pt: (b, 0, 0)),
                  pl.BlockSpec((1, n_new, QCAT), lambda b, pt: (b, 0, 0)),
                  pl.BlockSpec(memory_space=pl.ANY),
                  pl.BlockSpec(memory_space=pl.ANY)],
        out_specs=pl.BlockSpec((1, rows, KV_LORA), lambda b, pt: (b, 0, 0)),
        scratch_shapes=[pltpu.VMEM((2, chunk_pages * page, KV_LORA), F32),
                        pltpu.VMEM((2, chunk_pages * QK_ROPE, page), F32),
                        pltpu.SemaphoreType.DMA((2, 2))],
    )
    o = pl.pallas_call(
        functools.partial(_paged_kernel, n_new=n_new, layer=layer, page=page, chunk_pages=chunk_pages,
                          n_chunks=n_chunks, n_str=n_streams),
        grid_spec=grid_spec,
        out_shape=jax.ShapeDtypeStruct((n_seq, rows, KV_LORA), BF16),
        compiler_params=_params(("arbitrary",)),
        name="mla_sample_attn",
    )(page_table, q3, k3, cache_kv, cache_krt)
    return o.reshape(n_seq * n_new, H_B * KV_LORA)


def _gla_cumsum_matrix(chunk):
    blocks = [np.tril(np.ones((chunk, chunk), np.float32))]
    m = chunk // 2
    while m >= 1:
        w = np.zeros((chunk, chunk), np.float32)
        for t in range(chunk):
            seg = (t // m) * m
            if (t % (2 * m)) >= m:
                w[t, seg:t + 1] = 1.0
            else:
                w[t, t + 1:seg + m] = 1.0
        blocks.append(w)
        m //= 2
    return np.concatenate(blocks, axis=0)


def _gla_kernel(q_ref, k_ref, v_ref, g_ref, wc_ref, s0_ref, o_ref, sfin_ref, s_sc, *, chunk):
    c_idx = pl.program_id(1)

    @pl.when(c_idx == 0)
    def _():
        s_sc[...] = s0_ref[0]

    n_lev = chunk.bit_length() - 1
    row = lax.broadcasted_iota(jnp.int32, (chunk, chunk), 0)
    col = lax.broadcasted_iota(jnp.int32, (chunk, chunk), 1)
    row_in_chunk = lax.broadcasted_iota(jnp.int32, (chunk, DK_A), 0)
    sums_all = _dot_hi_exact_lhs(wc_ref[...], g_ref[...])
    for hh in range(H_A):
        q = q_ref[:, hh * DK_A:(hh + 1) * DK_A]
        k = k_ref[:, hh * DK_A:(hh + 1) * DK_A]
        v = v_ref[:, hh * DV_A:(hh + 1) * DV_A].astype(BF16)
        sums = sums_all[:, hh * DK_A:(hh + 1) * DK_A]
        b = sums[:chunk]
        b_last = b[chunk - 1:chunk, :]
        s_prev = s_sc[hh]

        a = jnp.where(row == col, _dot(q.astype(BF16), k.astype(BF16), NT), 0.0)
        for lev in range(n_lev):
            m = chunk >> (lev + 1)
            scale = jnp.exp(sums[(lev + 1) * chunk:(lev + 2) * chunk])
            upper = (row_in_chunk % (2 * m)) >= m
            q_l = jnp.where(upper, q * scale, 0.0).astype(BF16)
            k_l = jnp.where(upper, 0.0, k * scale).astype(BF16)
            same_block = (row // (2 * m)) == (col // (2 * m))
            a = a + jnp.where(same_block, _dot(q_l, k_l, NT), 0.0)

        inter = _dot((q * jnp.exp(b)).astype(BF16), s_prev.astype(BF16))
        o_ref[:, hh * DV_A:(hh + 1) * DV_A] = inter + _dot(a.astype(BF16), v)

        k_dec = (k * jnp.exp(b_last - b)).astype(BF16)
        decay_col = jnp.transpose(jnp.broadcast_to(jnp.exp(b_last), (DK_A, DK_A)))[:, :1]
        s_sc[hh] = decay_col * s_prev + _dot(k_dec, v, TN)

    @pl.when(c_idx == pl.num_programs(1) - 1)
    def _():
        sfin_ref[0] = s_sc[...]


def gla(qg, kg, vg, logf, s0, n_seq, seq, chunk):
    assert seq % chunk == 0 and chunk & (chunk - 1) == 0 and chunk % 8 == 0
    n_chunks = seq // chunk
    wc = jnp.asarray(_gla_cumsum_matrix(chunk), BF16)
    nk = H_A * DK_A

    def tok(cols):
        return pl.BlockSpec((chunk, cols), lambda b, c: (b * n_chunks + c, 0))

    state_spec = pl.BlockSpec((1, H_A, DK_A, DV_A), lambda b, c: (b, 0, 0, 0))
    return pl.pallas_call(
        functools.partial(_gla_kernel, chunk=chunk),
        grid=(n_seq, n_chunks),
        in_specs=[tok(nk), tok(nk), tok(H_A * DV_A), tok(nk), _const_spec(wc.shape), state_spec],
        out_specs=[tok(H_A * DV_A), state_spec],
        out_shape=[jax.ShapeDtypeStruct((n_seq * seq, H_A * DV_A), F32),
                   jax.ShapeDtypeStruct((n_seq, H_A, DK_A, DV_A), F32)],
        scratch_shapes=[pltpu.VMEM((H_A, DK_A, DV_A), F32)],
        compiler_params=_params(("arbitrary", "arbitrary")),
        name="gla",
    )(qg, kg, vg, logf, wc, s0)


def _route(h2, wrt_ref, rbias_ref, tm):
    logits = _dot_hi3(wrt_ref[...], h2, NT)
    scores = _sigmoid(logits)
    sel = scores + rbias_ref[...]
    neg = -jnp.inf

    iota_g = lax.broadcasted_iota(jnp.int32, (GROUP_SIZE, tm), 0).astype(F32)
    g_rows = []
    for g in range(N_GROUPS):
        blk = sel[g * GROUP_SIZE:(g + 1) * GROUP_SIZE]
        m1 = jnp.max(blk, axis=0, keepdims=True)
        first = jnp.min(jnp.where(blk == m1, iota_g, float(GROUP_SIZE)), axis=0, keepdims=True)
        m2 = jnp.max(jnp.where(iota_g == first, neg, blk), axis=0, keepdims=True)
        g_rows.append(m1 + m2)
    g_score = jnp.concatenate(g_rows, axis=0)

    iota_n = lax.broadcasted_iota(jnp.int32, (N_GROUPS, tm), 0).astype(F32)
    g_keep = jnp.zeros((N_GROUPS, tm), F32)
    for _ in range(TOPK_GROUPS):
        mx = jnp.max(g_score, axis=0, keepdims=True)
        first = jnp.min(jnp.where(g_score == mx, iota_n, float(N_GROUPS)), axis=0, keepdims=True)
        hit = iota_n == first
        g_keep = jnp.where(hit, 1.0, g_keep)
        g_score = jnp.where(hit, neg, g_score)

    sel_m = jnp.concatenate(
        [jnp.where(g_keep[g:g + 1] > 0.0, sel[g * GROUP_SIZE:(g + 1) * GROUP_SIZE], neg) for g in range(N_GROUPS)], axis=0)

    iota_e = lax.broadcasted_iota(jnp.int32, (N_EXPERTS, tm), 0).astype(F32)
    ids, wts, hits = [], [], []
    for _ in range(TOP_K):
        mx = jnp.max(sel_m, axis=0, keepdims=True)
        first = jnp.min(jnp.where(sel_m == mx, iota_e, float(N_EXPERTS)), axis=0, keepdims=True)
        hit = iota_e == first
        ids.append(first)
        hits.append(hit)
        wts.append(jnp.sum(jnp.where(hit, scores, 0.0), axis=0, keepdims=True))
        sel_m = jnp.where(hit, neg, sel_m)
    ids = jnp.concatenate(ids, axis=0)
    wts = jnp.concatenate(wts, axis=0)
    wts = wts / jnp.sum(wts, axis=0, keepdims=True) * ROUTED_SCALE
    return ids, wts, hits


def _rank_in_expert(hits, cnt_sc, tm):
    chosen = jnp.where(hits[0], 1.0, 0.0)
    for hit in hits[1:]:
        chosen = jnp.where(hit, 1.0, chosen)
    earlier = (lax.broadcasted_iota(jnp.int32, (tm, tm), 0) < lax.broadcasted_iota(jnp.int32, (tm, tm), 1))
    prefix = _dot(chosen.astype(BF16), jnp.where(earlier, 1.0, 0.0).astype(BF16))
    base = cnt_sc[:, :1] + prefix
    ranks = [jnp.sum(jnp.where(hit, base, 0.0), axis=0, keepdims=True) for hit in hits]
    cnt_sc[...] = cnt_sc[...] + jnp.sum(chosen, axis=1, keepdims=True)
    return jnp.concatenate(ranks, axis=0)


def _merge_kernel(x_ref, scm_ref, shm_ref, gm_ref, scf_ref, shf_ref, olat_ref, ogla_ref,
                  w3_ref, wuv_ref, wb_ref, wa_ref, wo_ref, ggla_ref, ln1g_ref, ln1b_ref, wrt_ref, rbias_ref, cnt0_ref,
                  x1_ref, h2_ref, ids_ref, wts_ref, rank_ref, cnt_ref, cnt_sc, *, alpha):
    @pl.when(pl.program_id(0) == 0)
    def _():
        cnt_sc[...] = cnt0_ref[...]

    gb, rb, d = x_ref.shape
    tm = gb * rb
    x = x_ref[...]
    h = (x * (1.0 + scm_ref[...]) + shm_ref[...]).reshape(tm, d).astype(BF16)
    z3 = _dot(h, w3_ref[...])

    vb = [_dot(olat_ref[0, hh], wuv_ref[hh]).astype(BF16) for hh in range(H_B)]
    y_b = _dot(jnp.concatenate(vb, axis=-1), wb_ref[...])

    ga = []
    for hh in range(H_A):
        o_n = _rms(ogla_ref[:, hh * DV_A:(hh + 1) * DV_A], ggla_ref[...])
        ga.append((o_n * _silu(z3[:, hh * DV_A:(hh + 1) * DV_A])).astype(BF16))
    y_a = _dot(jnp.concatenate(ga, axis=-1), wa_ref[...])

    merged = _sigmoid(z3[:, d:2 * d]) * y_a + _sigmoid(z3[:, 2 * d:]) * y_b
    mix = _dot(merged.astype(BF16), wo_ref[...]).reshape(gb, rb, d)
    x1 = _layer_norm(alpha * x + gm_ref[...] * mix, ln1g_ref[...], ln1b_ref[...])
    x1_ref[...] = x1
    h2 = (x1 * (1.0 + scf_ref[...]) + shf_ref[...]).reshape(tm, d)
    h2_ref[...] = h2.astype(BF16)
    ids, wts, hits = _route(h2, wrt_ref, rbias_ref, tm)
    ids_ref[...] = ids.astype(jnp.int32)
    wts_ref[...] = wts
    rank_ref[...] = _rank_in_expert(hits, cnt_sc, tm).astype(jnp.int32)
    cnt_ref[...] = cnt_sc[...]


def merge_and_route(x, mods, olat, ogla, mw, cnt0, alpha):
    g, r, d = x.shape
    t = g * r
    x_spec, mod_spec, n_tiles = _group_specs(g, r, d)
    names = ["w3", "wuv", "wb", "wa", "wo", "ggla", "ln1g", "ln1b", "wrt", "rbias"]
    kt_spec = pl.BlockSpec((TOP_K, TOKEN_TILE), lambda i: (0, i))
    cnt_spec = _const_spec((N_EXPERTS, LANES))
    return pl.pallas_call(
        functools.partial(_merge_kernel, alpha=alpha),
        grid=(n_tiles,),
        in_specs=[x_spec] + [mod_spec] * 5
                 + [pl.BlockSpec((1, H_B, TOKEN_TILE, KV_LORA), lambda i: (i, 0, 0, 0)), _tok_spec(H_A * DV_A)]
                 + [_const_spec(mw[n].shape) for n in names] + [cnt_spec],
        out_specs=[x_spec, _tok_spec(d), kt_spec, kt_spec, kt_spec, cnt_spec],
        out_shape=[jax.ShapeDtypeStruct((g, r, d), F32), jax.ShapeDtypeStruct((t, d), BF16),
                   jax.ShapeDtypeStruct((TOP_K, t), jnp.int32), jax.ShapeDtypeStruct((TOP_K, t), F32),
                   jax.ShapeDtypeStruct((TOP_K, t), jnp.int32), jax.ShapeDtypeStruct((N_EXPERTS, LANES), F32)],
        scratch_shapes=[pltpu.VMEM((N_EXPERTS, LANES), F32)],
        compiler_params=_params(("arbitrary",)),
        name="merge_route",
    )(x, *mods, olat, ogla, *[mw[n] for n in names], cnt0)


def _moe_kernel(pb_ref, pe_ref, lo_ref, hi_ref, x_ref, wg_ref, wu_ref, wd_ref, y_ref, wg_sc, wu_sc, wd_sc):
    j = pl.program_id(0)
    prev = jnp.maximum(j - 1, 0)
    new_expert = jnp.logical_or(j == 0, pe_ref[j] != pe_ref[prev])
    first_visit = jnp.logical_or(j == 0, pb_ref[j] != pb_ref[prev])
    lo = lo_ref[j]
    hi = hi_ref[j]

    @pl.when(new_expert)
    def _():
        wg_sc[...] = wg_ref[0].astype(BF16)
        wu_sc[...] = wu_ref[0].astype(BF16)
        wd_sc[...] = wd_ref[0].astype(BF16)

    @pl.when(first_visit)
    def _():
        y_ref[...] = jnp.zeros(y_ref.shape, y_ref.dtype)

    @pl.when(hi > lo)
    def _():
        x = x_ref[...]
        gate = _dot(x, wg_sc[...])
        up = _dot(x, wu_sc[...])
        act = (_silu(gate) * up).astype(BF16)
        y = _dot(act, wd_sc[...])
        row = lax.broadcasted_iota(jnp.int32, (y.shape[0], 1), 0)
        mine = jnp.logical_and(row >= lo, row < hi)
        y_ref[...] = jnp.where(mine, y, y_ref[...].astype(F32)).astype(y_ref.dtype)


def routed_expert_blocks(x_sorted, pairs, w_gate, w_up, w_down):
    m, d = x_sorted.shape
    de = w_gate.shape[-1]
    n_pairs = pairs[0].shape[0]
    grid_spec = pltpu.PrefetchScalarGridSpec(
        num_scalar_prefetch=4,
        grid=(n_pairs,),
        in_specs=[pl.BlockSpec((MOE_ROWS, d), lambda j, pb, pe, lo, hi: (pb[j], 0)),
                  pl.BlockSpec((1, d, de), lambda j, pb, pe, lo, hi: (pe[j], 0, 0)),
                  pl.BlockSpec((1, d, de), lambda j, pb, pe, lo, hi: (pe[j], 0, 0)),
                  pl.BlockSpec((1, de, d), lambda j, pb, pe, lo, hi: (pe[j], 0, 0))],
        out_specs=pl.BlockSpec((MOE_ROWS, d), lambda j, pb, pe, lo, hi: (pb[j], 0)),
        scratch_shapes=[pltpu.VMEM((d, de), BF16), pltpu.VMEM((d, de), BF16), pltpu.VMEM((de, d), BF16)],
    )
    return pl.pallas_call(
        _moe_kernel,
        grid_spec=grid_spec,
        out_shape=jax.ShapeDtypeStruct((m, d), BF16),
        compiler_params=_params(("arbitrary",)),
        name="moe_experts",
    )(*pairs, x_sorted, w_gate, w_up, w_down)


def _pos_kernel(ids_ref, rank_ref, pstart_ref, pos_ref):
    k, tm = ids_ref.shape
    iota_e = lax.broadcasted_iota(jnp.int32, (N_EXPERTS, tm), 0)
    pstart = pstart_ref[:, :1]
    base = [jnp.sum(jnp.where(iota_e == ids_ref[j:j + 1, :], pstart, 0.0), axis=0, keepdims=True) for j in range(k)]
    pos_ref[...] = jnp.concatenate(base, axis=0).astype(jnp.int32) + rank_ref[...]


def dispatch(ids, rank, counts):
    k, t = ids.shape
    m = k * t
    assert m % MOE_ROWS == 0
    n_blk = m // MOE_ROWS
    counts = counts[:, 0].astype(jnp.int32)
    uend = jnp.cumsum(counts)
    ustart = uend - counts
    pstart = jnp.broadcast_to(ustart.astype(F32)[:, None], (N_EXPERTS, LANES))
    kt_spec = pl.BlockSpec((k, TOKEN_TILE), lambda i: (0, i))
    pos = pl.pallas_call(
        _pos_kernel,
        grid=(t // TOKEN_TILE,),
        in_specs=[kt_spec, kt_spec, _const_spec((N_EXPERTS, LANES))],
        out_specs=kt_spec,
        out_shape=jax.ShapeDtypeStruct((k, t), jnp.int32),
        compiler_params=_params(("arbitrary",)),
        name="dispatch_positions",
    )(ids, rank, pstart)
    tok = jnp.broadcast_to(jnp.arange(t, dtype=jnp.int32)[None, :], (k, t))
    _, tok_sorted = lax.sort_key_val(pos.reshape(m), tok.reshape(m))

    first_blk = ustart // MOE_ROWS
    n_pairs_e = jnp.where(counts > 0, (uend - 1) // MOE_ROWS - first_blk + 1, 0)
    pair_end = jnp.cumsum(n_pairs_e)
    pair_start = pair_end - n_pairs_e
    total = pair_end[-1]
    j = jnp.arange(n_blk + N_EXPERTS, dtype=jnp.int32)
    valid = j < total
    e_j = jnp.minimum(jnp.searchsorted(pair_end, j, side="right"), N_EXPERTS - 1).astype(jnp.int32)
    e_j = jnp.where(valid, e_j, e_j[total - 1])
    b_j = jnp.where(valid, first_blk[e_j] + j - pair_start[e_j], n_blk - 1).astype(jnp.int32)
    lo = jnp.where(valid, jnp.maximum(ustart[e_j], b_j * MOE_ROWS) - b_j * MOE_ROWS, 0).astype(jnp.int32)
    hi = jnp.where(valid, jnp.minimum(uend[e_j], (b_j + 1) * MOE_ROWS) - b_j * MOE_ROWS, 0).astype(jnp.int32)
    return tok_sorted, pos, (b_j, e_j, lo, hi)


def _final_kernel(x1_ref, gf_ref, h2_ref, yg_ref, wts_ref, wsg_ref, wsu_ref, wsd_ref, ln2g_ref, ln2b_ref, y_ref, *, alpha):
    gb, rb, d = x1_ref.shape
    h2 = h2_ref[...]
    act = (_silu(_dot(h2, wsg_ref[...])) * _dot(h2, wsu_ref[...])).astype(BF16)
    ffn = _dot(act, wsd_ref[...])
    w = wts_ref[...]
    routed = yg_ref[0].astype(F32) * w[:, 0:1]
    for j in range(1, TOP_K):
        routed = routed + yg_ref[j].astype(F32) * w[:, j:j + 1]
    ffn = (ffn + routed).reshape(gb, rb, d)
    y_ref[...] = _layer_norm(alpha * x1_ref[...] + gf_ref[...] * ffn, ln2g_ref[...], ln2b_ref[...])


def shared_combine_norm(x1, gf, h2, yg, wts_t, fw, alpha):
    g, r, d = x1.shape
    x_spec, mod_spec, n_tiles = _group_specs(g, r, d)
    names = ["wsg", "wsu", "wsd", "ln2g", "ln2b"]
    return pl.pallas_call(
        functools.partial(_final_kernel, alpha=alpha),
        grid=(n_tiles,),
        in_specs=[x_spec, mod_spec, _tok_spec(d), pl.BlockSpec((TOP_K, TOKEN_TILE, d), lambda i: (0, i, 0)),
                  _tok_spec(TOP_K)] + [_const_spec(fw[n].shape) for n in names],
        out_specs=x_spec,
        out_shape=jax.ShapeDtypeStruct((g, r, d), F32),
        compiler_params=_params(("arbitrary",)),
        name="shared_combine_norm",
    )(x1, gf, h2, yg, wts_t, *[fw[n] for n in names])


def _rot_cols(w):
    half = w.shape[-1] // 2
    return jnp.concatenate([-w[..., half:], w[..., :half]], axis=-1)


def _pad_cols(w, width):
    return jnp.pad(w, [(0, 0)] * (w.ndim - 1) + [(0, width - w.shape[-1])])


def _prepare_weights(w):
    cuts = np.cumsum((0,) + IN_SIZES)
    part = [w["w_in"][:, cuts[i]:cuts[i + 1]] for i in range(len(IN_SIZES))]
    w_qa, w_kva, w_kr, w_gq, w_gk, w_gv, w_go, w_gf, w_ga, w_gb = part
    w1 = jnp.concatenate([w_qa, w_kva, _pad_cols(w_kr, LANES), _pad_cols(_rot_cols(w_kr), LANES)], axis=1)
    wuq = w["w_uq"].reshape(Q_LORA, H_B, QK_NOPE + QK_ROPE)
    wuq_rope = wuq[:, :, QK_NOPE:]
    wq = jnp.concatenate([wuq[:, :, :QK_NOPE].reshape(Q_LORA, H_B * QK_NOPE),
                          _pad_cols(wuq_rope, LANES).reshape(Q_LORA, H_B * LANES),
                          _pad_cols(_rot_cols(wuq_rope), LANES).reshape(Q_LORA, H_B * LANES)], axis=1)
    wg = jnp.concatenate([w_gq, w_gk, w_gv, _pad_cols(w_gf, LANES)], axis=1)
    proj = dict(
        w1=w1.astype(BF16), wq=wq.astype(BF16), wg=wg.astype(BF16),
        wuk=jnp.transpose(w["w_uk"], (1, 2, 0)).astype(BF16),
        wf2=jnp.pad(w["w_gla_f2"], ((0, LANES - GLA_LR), (0, 0))).astype(BF16),
        bf=w["b_gla_f"].reshape(1, -1), gq=w["g_q_norm"].reshape(1, -1), gkv=w["g_kv_norm"].reshape(1, -1))
    merge = dict(
        w3=jnp.concatenate([w_go, w_ga, w_gb], axis=1).astype(BF16),
        wuv=jnp.transpose(w["w_uv"], (1, 0, 2)).astype(BF16),
        wb=w["w_b_out"].astype(BF16), wa=w["w_a_out"].astype(BF16), wo=w["w_o"].astype(BF16),
        ggla=w["g_gla_norm"].reshape(1, -1), ln1g=w["ln1_g"].reshape(1, -1), ln1b=w["ln1_b"].reshape(1, -1),
        wrt=jnp.transpose(w["w_router"]), rbias=w["router_bias"].reshape(-1, 1))
    final = dict(
        wsg=w["w_s_gate"].astype(BF16), wsu=w["w_s_up"].astype(BF16), wsd=w["w_s_down"].astype(BF16),
        ln2g=w["ln2_g"].reshape(1, -1), ln2b=w["ln2_b"].reshape(1, -1))
    return proj, merge, final


def _rope_tables(pos, tile_rows):
    half = QK_ROPE // 2
    freqs = ROPE_THETA ** (-jnp.arange(half, dtype=F32) / half)
    ang = pos.astype(F32)[:, None] * freqs
    cos = _pad_cols(jnp.concatenate([jnp.cos(ang)] * 2, axis=-1), LANES)
    sin = _pad_cols(jnp.concatenate([jnp.sin(ang)] * 2, axis=-1), LANES)
    if pos.shape[0] < tile_rows:
        rep = tile_rows // pos.shape[0]
        cos, sin = jnp.tile(cos, (rep, 1)), jnp.tile(sin, (rep, 1))
    return cos, sin


def _split_mods(mod, n):
    return [m.reshape(n, 1, D_MODEL) for m in jnp.split(mod, 6, axis=-1)]


def kernel(x_prompt, x_sample, c_prompt, c_sample, cache_kv_latent, cache_k_rope, state_gla, page_table, w_ada, b_ada, w_in, g_q_norm, w_uq, g_kv_norm, w_uk, w_uv, w_gla_f2, b_gla_f, g_gla_norm, w_a_out, w_b_out, w_o, ln1_g, ln1_b, w_router, router_bias, w_e_gate, w_e_up, w_e_down, w_s_gate, w_s_up, w_s_down, ln2_g, ln2_b):
    w_all = dict(w_ada=w_ada, b_ada=b_ada, w_in=w_in, g_q_norm=g_q_norm, w_uq=w_uq, g_kv_norm=g_kv_norm, w_uk=w_uk,
                 w_uv=w_uv, w_gla_f2=w_gla_f2, b_gla_f=b_gla_f, g_gla_norm=g_gla_norm, w_a_out=w_a_out,
                 w_b_out=w_b_out, w_o=w_o, ln1_g=ln1_g, ln1_b=ln1_b, w_router=w_router, router_bias=router_bias,
                 w_e_gate=w_e_gate, w_e_up=w_e_up, w_e_down=w_e_down, w_s_gate=w_s_gate, w_s_up=w_s_up,
                 w_s_down=w_s_down, ln2_g=ln2_g, ln2_b=ln2_b)
    depth = w_ada.shape[0]
    alpha = (2.0 * depth) ** 0.25
    n_p, s_p, _ = x_prompt.shape
    n_s, s_s, _ = x_sample.shape
    t_p, t_s = n_p * s_p, n_s * s_s
    past_len = page_table.shape[1] * cache_kv_latent.shape[2]
    cache_krt = jnp.swapaxes(cache_k_rope, 2, 3)
    cos_p, sin_p = _rope_tables(jnp.arange(s_p, dtype=jnp.int32), TOKEN_TILE)
    cos_s, sin_s = _rope_tables(past_len + jnp.arange(s_s, dtype=jnp.int32), TOKEN_TILE)

    n_c = n_p + n_s
    n_c_pad = -(-n_c // 16) * 16
    c_all = jnp.pad(jnp.concatenate([c_prompt, c_sample], axis=0), ((0, n_c_pad - n_c), (0, 0)))

    yp, ys = x_prompt, x_sample
    outs = [[] for _ in range(6)]
    for layer in range(depth):
        w = {name: arr[layer] for name, arr in w_all.items()}
        pw, mw, fw = _prepare_weights(w)
        mod = adaln(c_all, w["w_ada"], w["b_ada"])
        mods_p = _split_mods(mod[:n_p], n_p)
        mods_s = _split_mods(mod[n_p:n_c], n_s)

        ckv_p, kr_p, kcat_p, kvt_p, qcat_p, qg_p, kg_p, vg_p, lf_p = input_projections(
            yp, mods_p[1], mods_p[0], cos_p, sin_p, pw)
        olat_p = mla_prompt_attention(qcat_p, kcat_p, kvt_p, n_p, s_p)
        s0_p = jnp.zeros((n_p, H_A, DK_A, DV_A), F32)
        ogla_p, sfin_p = gla(qg_p, kg_p, vg_p, lf_p, s0_p, n_p, s_p, min(GLA_CHUNK_PROMPT, s_p))
        cnt0 = jnp.zeros((N_EXPERTS, LANES), F32)
        x1_p, h2_p, ids_p, wts_p, rank_p, cnt_p = merge_and_route(
            yp, [mods_p[1], mods_p[0], mods_p[2], mods_p[4], mods_p[3]], olat_p, ogla_p, mw, cnt0, alpha)

        ckv_s, kr_s, kcat_s, _, qcat_s, qg_s, kg_s, vg_s, lf_s = input_projections(
            ys, mods_s[1], mods_s[0], cos_s, sin_s, pw)
        qtok_s = jnp.transpose(qcat_s, (0, 2, 1, 3)).reshape(t_s, H_B * QCAT)
        otok_s = mla_sample_attention(qtok_s, kcat_s, cache_kv_latent, cache_krt, page_table, layer, n_s, s_s)
        olat_s = jnp.transpose(otok_s.reshape(t_s // TOKEN_TILE, TOKEN_TILE, H_B, KV_LORA), (0, 2, 1, 3))
        ogla_s, sfin_s = gla(qg_s, kg_s, vg_s, lf_s, state_gla[layer], n_s, s_s, s_s)
        x1_s, h2_s, ids_s, wts_s, rank_s, cnt = merge_and_route(
            ys, [mods_s[1], mods_s[0], mods_s[2], mods_s[4], mods_s[3]], olat_s, ogla_s, mw, cnt_p, alpha)

        h2 = jnp.concatenate([h2_p, h2_s], axis=0)
        ids = jnp.concatenate([ids_p, ids_s], axis=1)
        rank = jnp.concatenate([rank_p, rank_s], axis=1)
        tok_sorted, pos, pairs = dispatch(ids, rank, cnt)
        x_sorted = h2.at[tok_sorted].get(mode="promise_in_bounds")
        y_sorted = routed_expert_blocks(x_sorted, pairs, w["w_e_gate"], w["w_e_up"], w["w_e_down"])
        yg_p = y_sorted.at[pos[:, :t_p]].get(mode="promise_in_bounds")
        yg_s = y_sorted.at[pos[:, t_p:]].get(mode="promise_in_bounds")

        yp = shared_combine_norm(x1_p, mods_p[5], h2_p, yg_p, jnp.transpose(wts_p), fw, alpha)
        ys = shared_combine_norm(x1_s, mods_s[5], h2_s, yg_s, jnp.transpose(wts_s), fw, alpha)

        for lst, val in zip(outs, [ckv_p.reshape(n_p, s_p, KV_LORA), kr_p.reshape(n_p, s_p, QK_ROPE), sfin_p,
                                   ckv_s.reshape(n_s, s_s, KV_LORA), kr_s.reshape(n_s, s_s, QK_ROPE), sfin_s]):
            lst.append(val.astype(state_gla.dtype) if val.ndim == 4 else val)
    return (yp, ys) + tuple(jnp.stack(o) for o in outs)
```

```python
import functools
import math

import numpy as np
import jax
import jax.numpy as jnp
from jax import lax
from jax.experimental import pallas as pl
from jax.experimental.pallas import tpu as pltpu

F32 = jnp.float32
BF16 = jnp.bfloat16

D_MODEL = 1024
H_A, DK_A, DV_A, GLA_LR, GLA_TAU = 4, 128, 256, 16, 16.0
H_B, Q_LORA, KV_LORA, QK_NOPE, QK_ROPE, V_HEAD = 8, 384, 256, 128, 64, 128
ROPE_THETA = 10000.0
ATTN_SCALE = (QK_NOPE + QK_ROPE) ** -0.5
Q_SCALE = ATTN_SCALE * 1.4426950408889634
N_EXPERTS, TOP_K, N_GROUPS, TOPK_GROUPS = 256, 8, 8, 4
GROUP_SIZE = N_EXPERTS // N_GROUPS
D_EXPERT, D_SHARED, ROUTED_SCALE = 256, 256, 2.5
EPS = 1e-6
IN_SIZES = (Q_LORA, KV_LORA, QK_ROPE, H_A * DK_A, H_A * DK_A, H_A * DV_A, H_A * DV_A, GLA_LR, D_MODEL, D_MODEL)

LANES = 128
QCAT = KV_LORA + LANES
VMEM_LIMIT = 56 * 1024 * 1024

TOKEN_TILE = 512
FLASH_TQ, FLASH_TK = 512, 512
GLA_CHUNK_PROMPT = 256
MOE_ROWS = 512

NN = (((1,), (0,)), ((), ()))
NT = (((1,), (1,)), ((), ()))
TN = (((0,), (0,)), ((), ()))


def _dot(a, b, dims=NN):
    return lax.dot_general(a, b, dims, preferred_element_type=F32)


def _split3(x):
    x1 = x.astype(BF16)
    r1 = x - x1.astype(F32)
    x2 = r1.astype(BF16)
    x3 = (r1 - x2.astype(F32)).astype(BF16)
    return x1, x2, x3


def _dot_hi(a, b, dims=NN):
    a1, a2, a3 = _split3(a)
    b1, b2, b3 = _split3(b)
    small = _dot(a3, b1, dims) + _dot(a2, b2, dims) + _dot(a1, b3, dims)
    mid = _dot(a2, b1, dims) + _dot(a1, b2, dims)
    return (small + mid) + _dot(a1, b1, dims)


def _dot_hi3(a, b, dims=NN):
    a1 = a.astype(BF16)
    a2 = (a - a1.astype(F32)).astype(BF16)
    b1 = b.astype(BF16)
    b2 = (b - b1.astype(F32)).astype(BF16)
    return (_dot(a2, b1, dims) + _dot(a1, b2, dims)) + _dot(a1, b1, dims)


def _dot_hi_exact_lhs(w, x, dims=NN):
    x1, x2, x3 = _split3(x)
    return (_dot(w, x3, dims) + _dot(w, x2, dims)) + _dot(w, x1, dims)


def _sigmoid(x):
    return 1.0 / (1.0 + jnp.exp(-x))


def _silu(x):
    return x * _sigmoid(x)


def _rms(x, g):
    return x * lax.rsqrt(jnp.mean(x * x, axis=-1, keepdims=True) + EPS) * g


def _layer_norm(x, g, b):
    mu = jnp.mean(x, axis=-1, keepdims=True)
    xc = x - mu
    var = jnp.mean(xc * xc, axis=-1, keepdims=True)
    return xc * lax.rsqrt(var + EPS) * g + b


def _params(sem):
    return pltpu.CompilerParams(dimension_semantics=sem, vmem_limit_bytes=VMEM_LIMIT)


def _const_spec(shape):
    nd = len(shape)
    return pl.BlockSpec(shape, lambda *_: (0,) * nd)


def _adaln_kernel(c_ref, w_ref, b_ref, o_ref):
    o_ref[...] = _dot_hi(_silu(c_ref[...]), w_ref[...]) + b_ref[...]


def adaln(c, w_ada, b_ada):
    n, d = c.shape
    e = w_ada.shape[1]
    tn = 512
    return pl.pallas_call(
        _adaln_kernel,
        grid=(e // tn,),
        in_specs=[_const_spec((n, d)), pl.BlockSpec((d, tn), lambda j: (0, j)), pl.BlockSpec((1, tn), lambda j: (0, j))],
        out_specs=pl.BlockSpec((n, tn), lambda j: (0, j)),
        out_shape=jax.ShapeDtypeStruct((n, e), F32),
        compiler_params=_params(("arbitrary",)),
        name="adaln",
    )(c, w_ada, b_ada.reshape(1, e))


W1_COLS = Q_LORA + KV_LORA + 2 * LANES
WQ_COLS = H_B * QK_NOPE + 2 * H_B * LANES
WG_COLS = 2 * H_A * DK_A + H_A * DV_A + LANES


def _proj_kernel(x_ref, sc_ref, sh_ref, cos_ref, sin_ref, w1_ref, wq_ref, wuk_ref, wg_ref, wf2_ref, bf_ref,
                 gq_ref, gkv_ref,
                 ckv_ref, krope_ref, kcat_ref, kvt_ref, qcat_ref, qg_ref, kg_ref, vg_ref, logf_ref):
    gb, rb, d = x_ref.shape
    tm = gb * rb
    h = (x_ref[...] * (1.0 + sc_ref[...]) + sh_ref[...]).reshape(tm, d).astype(BF16)
    cos = cos_ref[...]
    sin = sin_ref[...]

    z1 = _dot(h, w1_ref[...])
    ckv = _rms(z1[:, Q_LORA:Q_LORA + KV_LORA], gkv_ref[...])
    o_kr = Q_LORA + KV_LORA
    krope = z1[:, o_kr:o_kr + LANES] * cos + z1[:, o_kr + LANES:o_kr + 2 * LANES] * sin
    ckv_ref[...] = ckv
    krope_ref[...] = krope[:, :QK_ROPE]
    kcat_ref[:, :KV_LORA] = ckv.astype(BF16)
    kcat_ref[:, KV_LORA:] = krope.astype(BF16)
    kvt_ref[...] = jnp.transpose(ckv).astype(BF16)

    qn = _rms(z1[:, :Q_LORA], gq_ref[...]).astype(BF16)
    q2 = _dot(qn, wq_ref[...])
    o_r = H_B * QK_NOPE
    o_s = o_r + H_B * LANES
    for hh in range(H_B):
        q_nope = q2[:, hh * QK_NOPE:(hh + 1) * QK_NOPE].astype(BF16)
        q_lat = _dot(q_nope, wuk_ref[hh])
        q_rope = q2[:, o_r + hh * LANES:o_r + (hh + 1) * LANES] * cos + q2[:, o_s + hh * LANES:o_s + (hh + 1) * LANES] * sin
        qcat_ref[0, hh, :, :KV_LORA] = (q_lat * Q_SCALE).astype(BF16)
        qcat_ref[0, hh, :, KV_LORA:] = (q_rope * Q_SCALE).astype(BF16)

    z2 = _dot(h, wg_ref[...])
    nk = H_A * DK_A
    qg_ref[...] = z2[:, :nk] * DK_A ** -0.5
    kg_ref[...] = z2[:, nk:2 * nk]
    vg_ref[...] = z2[:, 2 * nk:2 * nk + H_A * DV_A]
    fa = z2[:, 2 * nk + H_A * DV_A:].astype(BF16)
    f_pre = _dot(fa, wf2_ref[...]) + bf_ref[...]
    log_sig = jnp.minimum(f_pre, 0.0) - jnp.log(1.0 + jnp.exp(-jnp.abs(f_pre)))
    logf_ref[...] = log_sig / GLA_TAU


def _group_tiling(g, r):
    if r >= TOKEN_TILE:
        assert r % TOKEN_TILE == 0
        return 1, TOKEN_TILE
    assert TOKEN_TILE % r == 0 and g % (TOKEN_TILE // r) == 0 and r % 8 == 0
    return TOKEN_TILE // r, r


def _group_specs(g, r, d):
    gb, rb = _group_tiling(g, r)
    nr = r // rb
    x_spec = pl.BlockSpec((gb, rb, d), lambda i: (i // nr, i % nr, 0))
    mod_spec = pl.BlockSpec((gb, 1, d), lambda i: (i // nr, 0, 0))
    return x_spec, mod_spec, (g // gb) * nr


def _tok_spec(cols):
    return pl.BlockSpec((TOKEN_TILE, cols), lambda i: (i, 0))


def input_projections(x, sc, sh, cos_tbl, sin_tbl, pw):
    g, r, d = x.shape
    t = g * r
    x_spec, mod_spec, n_tiles = _group_specs(g, r, d)
    n_tbl = cos_tbl.shape[0] // TOKEN_TILE
    tbl_spec = pl.BlockSpec((TOKEN_TILE, LANES), lambda i: (i % n_tbl, 0))
    nk = H_A * DK_A
    head_cols = [(KV_LORA, F32), (QK_ROPE, F32), (QCAT, BF16)]
    tail_cols = [(nk, F32), (nk, F32), (H_A * DV_A, F32), (nk, F32)]
    mid_specs = [pl.BlockSpec((KV_LORA, TOKEN_TILE), lambda i: (0, i)),
                 pl.BlockSpec((1, H_B, TOKEN_TILE, QCAT), lambda i: (i, 0, 0, 0))]
    mid_shapes = [jax.ShapeDtypeStruct((KV_LORA, t), BF16), jax.ShapeDtypeStruct((n_tiles, H_B, TOKEN_TILE, QCAT), BF16)]
    return pl.pallas_call(
        _proj_kernel,
        grid=(n_tiles,),
        in_specs=[x_spec, mod_spec, mod_spec, tbl_spec, tbl_spec,
                  _const_spec(pw["w1"].shape), _const_spec(pw["wq"].shape), _const_spec(pw["wuk"].shape),
                  _const_spec(pw["wg"].shape), _const_spec(pw["wf2"].shape), _const_spec(pw["bf"].shape),
                  _const_spec(pw["gq"].shape), _const_spec(pw["gkv"].shape)],
        out_specs=[_tok_spec(c) for c, _ in head_cols] + mid_specs + [_tok_spec(c) for c, _ in tail_cols],
        out_shape=[jax.ShapeDtypeStruct((t, c), dt) for c, dt in head_cols] + mid_shapes
                  + [jax.ShapeDtypeStruct((t, c), dt) for c, dt in tail_cols],
        compiler_params=_params(("arbitrary",)),
        name="input_proj",
    )(x, sc, sh, cos_tbl, sin_tbl, pw["w1"], pw["wq"], pw["wuk"], pw["wg"], pw["wf2"], pw["bf"], pw["gq"], pw["gkv"])


def _flash_kernel(qi_ref, ki_ref, q_ref, k_ref, vt_ref, o_ref, m_sc, l_sc, acc_sc, *, tq, tk):
    step = pl.program_id(1)
    qi = qi_ref[step]
    ki = ki_ref[step]
    n_tiles, _, tt, _ = q_ref.shape
    cols = n_tiles * H_B * tt

    @pl.when(ki == 0)
    def _():
        m_sc[...] = jnp.full(m_sc.shape, -jnp.inf, F32)
        l_sc[...] = jnp.zeros(l_sc.shape, F32)
        acc_sc[...] = jnp.zeros(acc_sc.shape, F32)

    def update(masked):
        st = _dot(k_ref[...], q_ref[...].reshape(cols, QCAT), NT)
        if masked:
            key = ki * tk + lax.broadcasted_iota(jnp.int32, (tk, cols), 0)
            col = lax.broadcasted_iota(jnp.int32, (tk, cols), 1)
            tok = qi * tq + (col // (H_B * tt)) * tt + col % tt
            st = jnp.where(key <= tok, st, -jnp.inf)
        m_prev = m_sc[...]
        m_new = jnp.maximum(m_prev, jnp.max(st, axis=0, keepdims=True))
        corr = jnp.exp2(m_prev - m_new)
        p = jnp.exp2(st - m_new)
        l_sc[...] = l_sc[...] * corr + jnp.sum(p, axis=0, keepdims=True)
        acc_sc[...] = acc_sc[...] * corr + _dot(vt_ref[...], p.astype(BF16))
        m_sc[...] = m_new

    crosses_diagonal = (ki + 1) * tk - 1 > qi * tq

    @pl.when(crosses_diagonal)
    def _():
        update(True)

    @pl.when(jnp.logical_not(crosses_diagonal))
    def _():
        update(False)

    @pl.when((ki + 1) * tk >= (qi + 1) * tq)
    def _():
        for u in range(n_tiles):
            for hh in range(H_B):
                cs = slice((u * H_B + hh) * tt, (u * H_B + hh + 1) * tt)
                o_ref[u, hh] = jnp.transpose(acc_sc[:, cs] / l_sc[:, cs]).astype(o_ref.dtype)


def mla_prompt_attention(qcat, kcat, kvt, n_seq, seq):
    tt = qcat.shape[2]
    n_tiles = max(1, min(FLASH_TQ, seq) // tt)
    tq, tk = n_tiles * tt, min(FLASH_TK, seq)
    assert seq % tq == 0 and seq % tk == 0
    nq, nkb = seq // tq, seq // tk
    rows = tq * H_B
    qi_list, ki_list = [], []
    for qi in range(nq):
        for ki in range(-(-((qi + 1) * tq) // tk)):
            qi_list.append(qi)
            ki_list.append(ki)
    qi_tbl = jnp.asarray(np.array(qi_list, np.int32))
    ki_tbl = jnp.asarray(np.array(ki_list, np.int32))
    grid_spec = pltpu.PrefetchScalarGridSpec(
        num_scalar_prefetch=2,
        grid=(n_seq, len(qi_list)),
        in_specs=[pl.BlockSpec((n_tiles, H_B, tt, QCAT), lambda b, s, qt, kt: (b * nq + qt[s], 0, 0, 0)),
                  pl.BlockSpec((tk, QCAT), lambda b, s, qt, kt: (b * nkb + kt[s], 0)),
                  pl.BlockSpec((KV_LORA, tk), lambda b, s, qt, kt: (0, b * nkb + kt[s]))],
        out_specs=pl.BlockSpec((n_tiles, H_B, tt, KV_LORA), lambda b, s, qt, kt: (b * nq + qt[s], 0, 0, 0)),
        scratch_shapes=[pltpu.VMEM((1, rows), F32), pltpu.VMEM((1, rows), F32), pltpu.VMEM((KV_LORA, rows), F32)],
    )
    return pl.pallas_call(
        functools.partial(_flash_kernel, tq=tq, tk=tk),
        grid_spec=grid_spec,
        out_shape=jax.ShapeDtypeStruct(qcat.shape[:3] + (KV_LORA,), BF16),
        compiler_params=_params(("arbitrary", "arbitrary")),
        name="mla_prompt_attn",
    )(qi_tbl, ki_tbl, qcat, kcat, kvt)


PAGES_PER_CHUNK = 64
PAGED_STREAMS = 4


def _paged_kernel(pt_ref, q_ref, knew_ref, ckv_hbm, ckrt_hbm, o_ref, kv_buf, krt_buf, sems, *,
                  n_new, layer, page, chunk_pages, n_chunks, n_str):
    b = pl.program_id(0)
    n_b = pl.num_programs(0)
    rows = n_new * H_B

    def page_copies(seq, chunk, slot, j):
        phys = pt_ref[seq, chunk * chunk_pages + j]
        dst = pl.ds(j * page, page)
        dst_t = pl.ds(j * QK_ROPE, QK_ROPE)
        return (pltpu.make_async_copy(ckv_hbm.at[layer, phys], kv_buf.at[slot, dst], sems.at[0, slot]),
                pltpu.make_async_copy(ckrt_hbm.at[layer, phys], krt_buf.at[slot, dst_t], sems.at[1, slot]))

    def start_chunk(seq, chunk, slot):
        for j in range(chunk_pages):
            for cp in page_copies(seq, chunk, slot, j):
                cp.start()

    def wait_chunk(seq, chunk, slot):
        for j in range(chunk_pages):
            for cp in page_copies(seq, chunk, slot, j):
                cp.wait()

    @pl.when(b == 0)
    def _():
        start_chunk(0, 0, 0)

    q = q_ref[0]
    qf = q.astype(F32)
    kn = knew_ref[0].astype(F32)
    tok = lax.broadcasted_iota(jnp.int32, (rows, 1), 0) // H_B
    cols = []
    for j in range(n_new):
        sj = jnp.sum(qf * kn[j:j + 1, :], axis=-1, keepdims=True)
        cols.append(jnp.where(j <= tok, sj, -jnp.inf))
    m = cols[0]
    for j in range(1, n_new):
        m = jnp.maximum(m, cols[j])
    l = jnp.zeros((rows, 1), F32)
    acc = jnp.zeros((rows, KV_LORA), F32)
    for j in range(n_new):
        pj = jnp.exp2(cols[j] - m)
        l = l + pj
        acc = acc + pj * kn[j:j + 1, :KV_LORA]

    q_lat = q[:, :KV_LORA]
    q_rope = q[:, KV_LORA:KV_LORA + QK_ROPE]
    sp = chunk_pages // n_str
    state = [(m, l, acc)] + [(jnp.full((rows, 1), -jnp.inf, F32), jnp.zeros((rows, 1), F32),
                              jnp.zeros((rows, KV_LORA), F32)) for _ in range(n_str - 1)]
    for c in range(n_chunks):
        slot = c % 2
        if c + 1 < n_chunks:
            start_chunk(b, c + 1, 1 - slot)
        else:
            @pl.when(b + 1 < n_b)
            def _():
                start_chunk(b + 1, 0, 1 - slot)
        wait_chunk(b, c, slot)
        for st in range(n_str):
            m, l, acc = state[st]
            kv = kv_buf[slot, st * sp * page:(st + 1) * sp * page, :].astype(BF16)
            s_rope = [_dot(q_rope, krt_buf[slot, j * QK_ROPE:(j + 1) * QK_ROPE, :].astype(BF16))
                      for j in range(st * sp, (st + 1) * sp)]
            s = _dot(q_lat, kv, NT) + jnp.concatenate(s_rope, axis=-1)
            m_new = jnp.maximum(m, jnp.max(s, axis=-1, keepdims=True))
            corr = jnp.exp2(m - m_new)
            p = jnp.exp2(s - m_new)
            l = l * corr + jnp.sum(p, axis=-1, keepdims=True)
            acc = acc * corr + _dot(p.astype(BF16), kv)
            state[st] = (m_new, l, acc)
    m = state[0][0]
    for st in range(1, n_str):
        m = jnp.maximum(m, state[st][0])
    l = jnp.zeros((rows, 1), F32)
    acc = jnp.zeros((rows, KV_LORA), F32)
    for m_s, l_s, acc_s in state:
        w = jnp.exp2(m_s - m)
        l = l + l_s * w
        acc = acc + acc_s * w
    o_ref[0] = (acc / l).astype(o_ref.dtype)


def mla_sample_attention(qcat, kcat, cache_kv, cache_krt, page_table, layer, n_seq, n_new):
    n_pages = page_table.shape[1]
    page = cache_kv.shape[2]
    rows = n_new * H_B
    chunk_pages = min(PAGES_PER_CHUNK, n_pages // 2)
    assert n_pages % (2 * chunk_pages) == 0
    n_streams = math.gcd(PAGED_STREAMS, chunk_pages)
    n_chunks = n_pages // chunk_pages
    q3 = qcat.reshape(n_seq, rows, QCAT)
    k3 = kcat.reshape(n_seq, n_new, QCAT)
    grid_spec = pltpu.PrefetchScalarGridSpec(
        num_scalar_prefetch=1,
        grid=(n_seq,),
        in_specs=[pl.BlockSpec((1, rows, QCAT), lambda b, pt: (b, 0, 0)),
                  pl.BlockSpec((1, n_new, QCAT), lambda b, pt: (b, 0, 0)),
                  pl.BlockSpec(memory_space=pl.ANY),
                  pl.BlockSpec(memory_space=pl.ANY)],
        out_specs=pl.BlockSpec((1, rows, KV_LORA), lambda b, pt: (b, 0, 0)),
        scratch_shapes=[pltpu.VMEM((2, chunk_pages * page, KV_LORA), F32),
                        pltpu.VMEM((2, chunk_pages * QK_ROPE, page), F32),
                        pltpu.SemaphoreType.DMA((2, 2))],
    )
    o = pl.pallas_call(
        functools.partial(_paged_kernel, n_new=n_new, layer=layer, page=page, chunk_pages=chunk_pages,
                          n_chunks=n_chunks, n_str=n_streams),
        grid_spec=grid_spec,
        out_shape=jax.ShapeDtypeStruct((n_seq, rows, KV_LORA), BF16),
        compiler_params=_params(("arbitrary",)),
        name="mla_sample_attn",
    )(page_table, q3, k3, cache_kv, cache_krt)
    return o.reshape(n_seq * n_new, H_B * KV_LORA)


def _gla_cumsum_matrix(chunk):
    blocks = [np.tril(np.ones((chunk, chunk), np.float32))]
    m = chunk // 2
    while m >= 1:
        w = np.zeros((chunk, chunk), np.float32)
        for t in range(chunk):
            seg = (t // m) * m
            if (t % (2 * m)) >= m:
                w[t, seg:t + 1] = 1.0
            else:
                w[t, t + 1:seg + m] = 1.0
        blocks.append(w)
        m //= 2
    return np.concatenate(blocks, axis=0)


def _gla_kernel(q_ref, k_ref, v_ref, g_ref, wc_ref, s0_ref, o_ref, sfin_ref, s_sc, *, chunk):
    c_idx = pl.program_id(1)

    @pl.when(c_idx == 0)
    def _():
        s_sc[...] = s0_ref[0]

    n_lev = chunk.bit_length() - 1
    row = lax.broadcasted_iota(jnp.int32, (chunk, chunk), 0)
    col = lax.broadcasted_iota(jnp.int32, (chunk, chunk), 1)
    row_in_chunk = lax.broadcasted_iota(jnp.int32, (chunk, DK_A), 0)
    sums_all = _dot_hi_exact_lhs(wc_ref[...], g_ref[...])
    for hh in range(H_A):
        q = q_ref[:, hh * DK_A:(hh + 1) * DK_A]
        k = k_ref[:, hh * DK_A:(hh + 1) * DK_A]
        v = v_ref[:, hh * DV_A:(hh + 1) * DV_A].astype(BF16)
        sums = sums_all[:, hh * DK_A:(hh + 1) * DK_A]
        b = sums[:chunk]
        b_last = b[chunk - 1:chunk, :]
        s_prev = s_sc[hh]

        a = jnp.where(row == col, _dot(q.astype(BF16), k.astype(BF16), NT), 0.0)
        for lev in range(n_lev):
            m = chunk >> (lev + 1)
            scale = jnp.exp(sums[(lev + 1) * chunk:(lev + 2) * chunk])
            upper = (row_in_chunk % (2 * m)) >= m
            q_l = jnp.where(upper, q * scale, 0.0).astype(BF16)
            k_l = jnp.where(upper, 0.0, k * scale).astype(BF16)
            same_block = (row // (2 * m)) == (col // (2 * m))
            a = a + jnp.where(same_block, _dot(q_l, k_l, NT), 0.0)

        inter = _dot((q * jnp.exp(b)).astype(BF16), s_prev.astype(BF16))
        o_ref[:, hh * DV_A:(hh + 1) * DV_A] = inter + _dot(a.astype(BF16), v)

        k_dec = (k * jnp.exp(b_last - b)).astype(BF16)
        decay_col = jnp.transpose(jnp.broadcast_to(jnp.exp(b_last), (DK_A, DK_A)))[:, :1]
        s_sc[hh] = decay_col * s_prev + _dot(k_dec, v, TN)

    @pl.when(c_idx == pl.num_programs(1) - 1)
    def _():
        sfin_ref[0] = s_sc[...]


def gla(qg, kg, vg, logf, s0, n_seq, seq, chunk):
    assert seq % chunk == 0 and chunk & (chunk - 1) == 0 and chunk % 8 == 0
    n_chunks = seq // chunk
    wc = jnp.asarray(_gla_cumsum_matrix(chunk), BF16)
    nk = H_A * DK_A

    def tok(cols):
        return pl.BlockSpec((chunk, cols), lambda b, c: (b * n_chunks + c, 0))

    state_spec = pl.BlockSpec((1, H_A, DK_A, DV_A), lambda b, c: (b, 0, 0, 0))
    return pl.pallas_call(
        functools.partial(_gla_kernel, chunk=chunk),
        grid=(n_seq, n_chunks),
        in_specs=[tok(nk), tok(nk), tok(H_A * DV_A), tok(nk), _const_spec(wc.shape), state_spec],
        out_specs=[tok(H_A * DV_A), state_spec],
        out_shape=[jax.ShapeDtypeStruct((n_seq * seq, H_A * DV_A), F32),
                   jax.ShapeDtypeStruct((n_seq, H_A, DK_A, DV_A), F32)],
        scratch_shapes=[pltpu.VMEM((H_A, DK_A, DV_A), F32)],
        compiler_params=_params(("arbitrary", "arbitrary")),
        name="gla",
    )(qg, kg, vg, logf, wc, s0)


def _route(h2, wrt_ref, rbias_ref, tm):
    logits = _dot_hi3(wrt_ref[...], h2, NT)
    scores = _sigmoid(logits)
    sel = scores + rbias_ref[...]
    neg = -jnp.inf

    iota_g = lax.broadcasted_iota(jnp.int32, (GROUP_SIZE, tm), 0).astype(F32)
    g_rows = []
    for g in range(N_GROUPS):
        blk = sel[g * GROUP_SIZE:(g + 1) * GROUP_SIZE]
        m1 = jnp.max(blk, axis=0, keepdims=True)
        first = jnp.min(jnp.where(blk == m1, iota_g, float(GROUP_SIZE)), axis=0, keepdims=True)
        m2 = jnp.max(jnp.where(iota_g == first, neg, blk), axis=0, keepdims=True)
        g_rows.append(m1 + m2)
    g_score = jnp.concatenate(g_rows, axis=0)

    iota_n = lax.broadcasted_iota(jnp.int32, (N_GROUPS, tm), 0).astype(F32)
    g_keep = jnp.zeros((N_GROUPS, tm), F32)
    for _ in range(TOPK_GROUPS):
        mx = jnp.max(g_score, axis=0, keepdims=True)
        first = jnp.min(jnp.where(g_score == mx, iota_n, float(N_GROUPS)), axis=0, keepdims=True)
        hit = iota_n == first
        g_keep = jnp.where(hit, 1.0, g_keep)
        g_score = jnp.where(hit, neg, g_score)

    sel_m = jnp.concatenate(
        [jnp.where(g_keep[g:g + 1] > 0.0, sel[g * GROUP_SIZE:(g + 1) * GROUP_SIZE], neg) for g in range(N_GROUPS)], axis=0)

    iota_e = lax.broadcasted_iota(jnp.int32, (N_EXPERTS, tm), 0).astype(F32)
    ids, wts, hits = [], [], []
    for _ in range(TOP_K):
        mx = jnp.max(sel_m, axis=0, keepdims=True)
        first = jnp.min(jnp.where(sel_m == mx, iota_e, float(N_EXPERTS)), axis=0, keepdims=True)
        hit = iota_e == first
        ids.append(first)
        hits.append(hit)
        wts.append(jnp.sum(jnp.where(hit, scores, 0.0), axis=0, keepdims=True))
        sel_m = jnp.where(hit, neg, sel_m)
    ids = jnp.concatenate(ids, axis=0)
    wts = jnp.concatenate(wts, axis=0)
    wts = wts / jnp.sum(wts, axis=0, keepdims=True) * ROUTED_SCALE
    return ids, wts, hits


def _rank_in_expert(hits, cnt_sc, tm):
    chosen = jnp.where(hits[0], 1.0, 0.0)
    for hit in hits[1:]:
        chosen = jnp.where(hit, 1.0, chosen)
    earlier = (lax.broadcasted_iota(jnp.int32, (tm, tm), 0) < lax.broadcasted_iota(jnp.int32, (tm, tm), 1))
    prefix = _dot(chosen.astype(BF16), jnp.where(earlier, 1.0, 0.0).astype(BF16))
    base = cnt_sc[:, :1] + prefix
    ranks = [jnp.sum(jnp.where(hit, base, 0.0), axis=0, keepdims=True) for hit in hits]
    cnt_sc[...] = cnt_sc[...] + jnp.sum(chosen, axis=1, keepdims=True)
    return jnp.concatenate(ranks, axis=0)


def _merge_kernel(x_ref, scm_ref, shm_ref, gm_ref, scf_ref, shf_ref, olat_ref, ogla_ref,
                  w3_ref, wuv_ref, wb_ref, wa_ref, wo_ref, ggla_ref, ln1g_ref, ln1b_ref, wrt_ref, rbias_ref, cnt0_ref,
                  x1_ref, h2_ref, ids_ref, wts_ref, rank_ref, cnt_ref, cnt_sc, *, alpha):
    @pl.when(pl.program_id(0) == 0)
    def _():
        cnt_sc[...] = cnt0_ref[...]

    gb, rb, d = x_ref.shape
    tm = gb * rb
    x = x_ref[...]
    h = (x * (1.0 + scm_ref[...]) + shm_ref[...]).reshape(tm, d).astype(BF16)
    z3 = _dot(h, w3_ref[...])

    vb = [_dot(olat_ref[0, hh], wuv_ref[hh]).astype(BF16) for hh in range(H_B)]
    y_b = _dot(jnp.concatenate(vb, axis=-1), wb_ref[...])

    ga = []
    for hh in range(H_A):
        o_n = _rms(ogla_ref[:, hh * DV_A:(hh + 1) * DV_A], ggla_ref[...])
        ga.append((o_n * _silu(z3[:, hh * DV_A:(hh + 1) * DV_A])).astype(BF16))
    y_a = _dot(jnp.concatenate(ga, axis=-1), wa_ref[...])

    merged = _sigmoid(z3[:, d:2 * d]) * y_a + _sigmoid(z3[:, 2 * d:]) * y_b
    mix = _dot(merged.astype(BF16), wo_ref[...]).reshape(gb, rb, d)
    x1 = _layer_norm(alpha * x + gm_ref[...] * mix, ln1g_ref[...], ln1b_ref[...])
    x1_ref[...] = x1
    h2 = (x1 * (1.0 + scf_ref[...]) + shf_ref[...]).reshape(tm, d)
    h2_ref[...] = h2.astype(BF16)
    ids, wts, hits = _route(h2, wrt_ref, rbias_ref, tm)
    ids_ref[...] = ids.astype(jnp.int32)
    wts_ref[...] = wts
    rank_ref[...] = _rank_in_expert(hits, cnt_sc, tm).astype(jnp.int32)
    cnt_ref[...] = cnt_sc[...]


def merge_and_route(x, mods, olat, ogla, mw, cnt0, alpha):
    g, r, d = x.shape
    t = g * r
    x_spec, mod_spec, n_tiles = _group_specs(g, r, d)
    names = ["w3", "wuv", "wb", "wa", "wo", "ggla", "ln1g", "ln1b", "wrt", "rbias"]
    kt_spec = pl.BlockSpec((TOP_K, TOKEN_TILE), lambda i: (0, i))
    cnt_spec = _const_spec((N_EXPERTS, LANES))
    return pl.pallas_call(
        functools.partial(_merge_kernel, alpha=alpha),
        grid=(n_tiles,),
        in_specs=[x_spec] + [mod_spec] * 5
                 + [pl.BlockSpec((1, H_B, TOKEN_TILE, KV_LORA), lambda i: (i, 0, 0, 0)), _tok_spec(H_A * DV_A)]
                 + [_const_spec(mw[n].shape) for n in names] + [cnt_spec],
        out_specs=[x_spec, _tok_spec(d), kt_spec, kt_spec, kt_spec, cnt_spec],
        out_shape=[jax.ShapeDtypeStruct((g, r, d), F32), jax.ShapeDtypeStruct((t, d), BF16),
                   jax.ShapeDtypeStruct((TOP_K, t), jnp.int32), jax.ShapeDtypeStruct((TOP_K, t), F32),
                   jax.ShapeDtypeStruct((TOP_K, t), jnp.int32), jax.ShapeDtypeStruct((N_EXPERTS, LANES), F32)],
        scratch_shapes=[pltpu.VMEM((N_EXPERTS, LANES), F32)],
        compiler_params=_params(("arbitrary",)),
        name="merge_route",
    )(x, *mods, olat, ogla, *[mw[n] for n in names], cnt0)


def _moe_kernel(pb_ref, pe_ref, lo_ref, hi_ref, x_ref, wg_ref, wu_ref, wd_ref, y_ref, wg_sc, wu_sc, wd_sc):
    j = pl.program_id(0)
    prev = jnp.maximum(j - 1, 0)
    new_expert = jnp.logical_or(j == 0, pe_ref[j] != pe_ref[prev])
    first_visit = jnp.logical_or(j == 0, pb_ref[j] != pb_ref[prev])
    lo = lo_ref[j]
    hi = hi_ref[j]

    @pl.when(new_expert)
    def _():
        wg_sc[...] = wg_ref[0].astype(BF16)
        wu_sc[...] = wu_ref[0].astype(BF16)
        wd_sc[...] = wd_ref[0].astype(BF16)

    @pl.when(first_visit)
    def _():
        y_ref[...] = jnp.zeros(y_ref.shape, y_ref.dtype)

    @pl.when(hi > lo)
    def _():
        x = x_ref[...]
        gate = _dot(x, wg_sc[...])
        up = _dot(x, wu_sc[...])
        act = (_silu(gate) * up).astype(BF16)
        y = _dot(act, wd_sc[...])
        row = lax.broadcasted_iota(jnp.int32, (y.shape[0], 1), 0)
        mine = jnp.logical_and(row >= lo, row < hi)
        y_ref[...] = jnp.where(mine, y, y_ref[...].astype(F32)).astype(y_ref.dtype)


def routed_expert_blocks(x_sorted, pairs, w_gate, w_up, w_down):
    m, d = x_sorted.shape
    de = w_gate.shape[-1]
    n_pairs = pairs[0].shape[0]
    grid_spec = pltpu.PrefetchScalarGridSpec(
        num_scalar_prefetch=4,
        grid=(n_pairs,),
        in_specs=[pl.BlockSpec((MOE_ROWS, d), lambda j, pb, pe, lo, hi: (pb[j], 0)),
                  pl.BlockSpec((1, d, de), lambda j, pb, pe, lo, hi: (pe[j], 0, 0)),
                  pl.BlockSpec((1, d, de), lambda j, pb, pe, lo, hi: (pe[j], 0, 0)),
                  pl.BlockSpec((1, de, d), lambda j, pb, pe, lo, hi: (pe[j], 0, 0))],
        out_specs=pl.BlockSpec((MOE_ROWS, d), lambda j, pb, pe, lo, hi: (pb[j], 0)),
        scratch_shapes=[pltpu.VMEM((d, de), BF16), pltpu.VMEM((d, de), BF16), pltpu.VMEM((de, d), BF16)],
    )
    return pl.pallas_call(
        _moe_kernel,
        grid_spec=grid_spec,
        out_shape=jax.ShapeDtypeStruct((m, d), BF16),
        compiler_params=_params(("arbitrary",)),
        name="moe_experts",
    )(*pairs, x_sorted, w_gate, w_up, w_down)


def _pos_kernel(ids_ref, rank_ref, pstart_ref, pos_ref):
    k, tm = ids_ref.shape
    iota_e = lax.broadcasted_iota(jnp.int32, (N_EXPERTS, tm), 0)
    pstart = pstart_ref[:, :1]
    base = [jnp.sum(jnp.where(iota_e == ids_ref[j:j + 1, :], pstart, 0.0), axis=0, keepdims=True) for j in range(k)]
    pos_ref[...] = jnp.concatenate(base, axis=0).astype(jnp.int32) + rank_ref[...]


def dispatch(ids, rank, counts):
    k, t = ids.shape
    m = k * t
    assert m % MOE_ROWS == 0
    n_blk = m // MOE_ROWS
    counts = counts[:, 0].astype(jnp.int32)
    uend = jnp.cumsum(counts)
    ustart = uend - counts
    pstart = jnp.broadcast_to(ustart.astype(F32)[:, None], (N_EXPERTS, LANES))
    kt_spec = pl.BlockSpec((k, TOKEN_TILE), lambda i: (0, i))
    pos = pl.pallas_call(
        _pos_kernel,
        grid=(t // TOKEN_TILE,),
        in_specs=[kt_spec, kt_spec, _const_spec((N_EXPERTS, LANES))],
        out_specs=kt_spec,
        out_shape=jax.ShapeDtypeStruct((k, t), jnp.int32),
        compiler_params=_params(("arbitrary",)),
        name="dispatch_positions",
    )(ids, rank, pstart)
    tok = jnp.broadcast_to(jnp.arange(t, dtype=jnp.int32)[None, :], (k, t))
    _, tok_sorted = lax.sort_key_val(pos.reshape(m), tok.reshape(m))

    first_blk = ustart // MOE_ROWS
    n_pairs_e = jnp.where(counts > 0, (uend - 1) // MOE_ROWS - first_blk + 1, 0)
    pair_end = jnp.cumsum(n_pairs_e)
    pair_start = pair_end - n_pairs_e
    total = pair_end[-1]
    j = jnp.arange(n_blk + N_EXPERTS, dtype=jnp.int32)
    valid = j < total
    e_j = jnp.minimum(jnp.searchsorted(pair_end, j, side="right"), N_EXPERTS - 1).astype(jnp.int32)
    e_j = jnp.where(valid, e_j, e_j[total - 1])
    b_j = jnp.where(valid, first_blk[e_j] + j - pair_start[e_j], n_blk - 1).astype(jnp.int32)
    lo = jnp.where(valid, jnp.maximum(ustart[e_j], b_j * MOE_ROWS) - b_j * MOE_ROWS, 0).astype(jnp.int32)
    hi = jnp.where(valid, jnp.minimum(uend[e_j], (b_j + 1) * MOE_ROWS) - b_j * MOE_ROWS, 0).astype(jnp.int32)
    return tok_sorted, pos, (b_j, e_j, lo, hi)


def _shared_kernel(h2_ref, wsg_ref, wsu_ref, wsd_ref, o_ref):
    h2 = h2_ref[...]
    act = (_silu(_dot(h2, wsg_ref[...])) * _dot(h2, wsu_ref[...])).astype(BF16)
    o_ref[...] = _dot(act, wsd_ref[...])


def shared_expert(h2, fw):
    t, d = h2.shape
    names = ["wsg", "wsu", "wsd"]
    return pl.pallas_call(
        _shared_kernel,
        grid=(t // TOKEN_TILE,),
        in_specs=[_tok_spec(d)] + [_const_spec(fw[n].shape) for n in names],
        out_specs=_tok_spec(d),
        out_shape=jax.ShapeDtypeStruct((t, d), F32),
        compiler_params=_params(("arbitrary",)),
        name="shared_expert",
    )(h2, *[fw[n] for n in names])


def _final_kernel(x1_ref, gf_ref, sh_ref, yg_ref, wts_ref, ln2g_ref, ln2b_ref, y_ref, *, alpha):
    gb, rb, d = x1_ref.shape
    w = wts_ref[...]
    routed = yg_ref[0].astype(F32) * w[:, 0:1]
    for j in range(1, TOP_K):
        routed = routed + yg_ref[j].astype(F32) * w[:, j:j + 1]
    ffn = (sh_ref[...] + routed).reshape(gb, rb, d)
    y_ref[...] = _layer_norm(alpha * x1_ref[...] + gf_ref[...] * ffn, ln2g_ref[...], ln2b_ref[...])


def combine_norm(x1, gf, shared, yg, wts_t, fw, alpha):
    g, r, d = x1.shape
    x_spec, mod_spec, n_tiles = _group_specs(g, r, d)
    names = ["ln2g", "ln2b"]
    return pl.pallas_call(
        functools.partial(_final_kernel, alpha=alpha),
        grid=(n_tiles,),
        in_specs=[x_spec, mod_spec, _tok_spec(d), pl.BlockSpec((TOP_K, TOKEN_TILE, d), lambda i: (0, i, 0)),
                  _tok_spec(TOP_K)] + [_const_spec(fw[n].shape) for n in names],
        out_specs=x_spec,
        out_shape=jax.ShapeDtypeStruct((g, r, d), F32),
        compiler_params=_params(("arbitrary",)),
        name="combine_norm",
    )(x1, gf, shared, yg, wts_t, *[fw[n] for n in names])


def _rot_cols(w):
    half = w.shape[-1] // 2
    return jnp.concatenate([-w[..., half:], w[..., :half]], axis=-1)


def _pad_cols(w, width):
    return jnp.pad(w, [(0, 0)] * (w.ndim - 1) + [(0, width - w.shape[-1])])


def _prepare_weights(w):
    cuts = np.cumsum((0,) + IN_SIZES)
    part = [w["w_in"][:, cuts[i]:cuts[i + 1]] for i in range(len(IN_SIZES))]
    w_qa, w_kva, w_kr, w_gq, w_gk, w_gv, w_go, w_gf, w_ga, w_gb = part
    w1 = jnp.concatenate([w_qa, w_kva, _pad_cols(w_kr, LANES), _pad_cols(_rot_cols(w_kr), LANES)], axis=1)
    wuq = w["w_uq"].reshape(Q_LORA, H_B, QK_NOPE + QK_ROPE)
    wuq_rope = wuq[:, :, QK_NOPE:]
    wq = jnp.concatenate([wuq[:, :, :QK_NOPE].reshape(Q_LORA, H_B * QK_NOPE),
                          _pad_cols(wuq_rope, LANES).reshape(Q_LORA, H_B * LANES),
                          _pad_cols(_rot_cols(wuq_rope), LANES).reshape(Q_LORA, H_B * LANES)], axis=1)
    wg = jnp.concatenate([w_gq, w_gk, w_gv, _pad_cols(w_gf, LANES)], axis=1)
    proj = dict(
        w1=w1.astype(BF16), wq=wq.astype(BF16), wg=wg.astype(BF16),
        wuk=jnp.transpose(w["w_uk"], (1, 2, 0)).astype(BF16),
        wf2=jnp.pad(w["w_gla_f2"], ((0, LANES - GLA_LR), (0, 0))).astype(BF16),
        bf=w["b_gla_f"].reshape(1, -1), gq=w["g_q_norm"].reshape(1, -1), gkv=w["g_kv_norm"].reshape(1, -1))
    merge = dict(
        w3=jnp.concatenate([w_go, w_ga, w_gb], axis=1).astype(BF16),
        wuv=jnp.transpose(w["w_uv"], (1, 0, 2)).astype(BF16),
        wb=w["w_b_out"].astype(BF16), wa=w["w_a_out"].astype(BF16), wo=w["w_o"].astype(BF16),
        ggla=w["g_gla_norm"].reshape(1, -1), ln1g=w["ln1_g"].reshape(1, -1), ln1b=w["ln1_b"].reshape(1, -1),
        wrt=jnp.transpose(w["w_router"]), rbias=w["router_bias"].reshape(-1, 1))
    final = dict(
        wsg=w["w_s_gate"].astype(BF16), wsu=w["w_s_up"].astype(BF16), wsd=w["w_s_down"].astype(BF16),
        ln2g=w["ln2_g"].reshape(1, -1), ln2b=w["ln2_b"].reshape(1, -1))
    return proj, merge, final


def _rope_tables(pos, tile_rows):
    half = QK_ROPE // 2
    freqs = ROPE_THETA ** (-jnp.arange(half, dtype=F32) / half)
    ang = pos.astype(F32)[:, None] * freqs
    cos = _pad_cols(jnp.concatenate([jnp.cos(ang)] * 2, axis=-1), LANES)
    sin = _pad_cols(jnp.concatenate([jnp.sin(ang)] * 2, axis=-1), LANES)
    if pos.shape[0] < tile_rows:
        rep = tile_rows // pos.shape[0]
        cos, sin = jnp.tile(cos, (rep, 1)), jnp.tile(sin, (rep, 1))
    return cos, sin


def _split_mods(mod, n):
    return [m.reshape(n, 1, D_MODEL) for m in jnp.split(mod, 6, axis=-1)]


def kernel(x_prompt, x_sample, c_prompt, c_sample, cache_kv_latent, cache_k_rope, state_gla, page_table, w_ada, b_ada, w_in, g_q_norm, w_uq, g_kv_norm, w_uk, w_uv, w_gla_f2, b_gla_f, g_gla_norm, w_a_out, w_b_out, w_o, ln1_g, ln1_b, w_router, router_bias, w_e_gate, w_e_up, w_e_down, w_s_gate, w_s_up, w_s_down, ln2_g, ln2_b):
    w_all = dict(w_ada=w_ada, b_ada=b_ada, w_in=w_in, g_q_norm=g_q_norm, w_uq=w_uq, g_kv_norm=g_kv_norm, w_uk=w_uk,
                 w_uv=w_uv, w_gla_f2=w_gla_f2, b_gla_f=b_gla_f, g_gla_norm=g_gla_norm, w_a_out=w_a_out,
                 w_b_out=w_b_out, w_o=w_o, ln1_g=ln1_g, ln1_b=ln1_b, w_router=w_router, router_bias=router_bias,
                 w_e_gate=w_e_gate, w_e_up=w_e_up, w_e_down=w_e_down, w_s_gate=w_s_gate, w_s_up=w_s_up,
                 w_s_down=w_s_down, ln2_g=ln2_g, ln2_b=ln2_b)
    depth = w_ada.shape[0]
    alpha = (2.0 * depth) ** 0.25
    n_p, s_p, _ = x_prompt.shape
    n_s, s_s, _ = x_sample.shape
    t_p, t_s = n_p * s_p, n_s * s_s
    past_len = page_table.shape[1] * cache_kv_latent.shape[2]
    cache_krt = jnp.swapaxes(cache_k_rope, 2, 3)
    cos_p, sin_p = _rope_tables(jnp.arange(s_p, dtype=jnp.int32), TOKEN_TILE)
    cos_s, sin_s = _rope_tables(past_len + jnp.arange(s_s, dtype=jnp.int32), TOKEN_TILE)

    n_c = n_p + n_s
    n_c_pad = -(-n_c // 16) * 16
    c_all = jnp.pad(jnp.concatenate([c_prompt, c_sample], axis=0), ((0, n_c_pad - n_c), (0, 0)))

    yp, ys = x_prompt, x_sample
    outs = [[] for _ in range(6)]
    for layer in range(depth):
        w = {name: arr[layer] for name, arr in w_all.items()}
        pw, mw, fw = _prepare_weights(w)
        mod = adaln(c_all, w["w_ada"], w["b_ada"])
        mods_p = _split_mods(mod[:n_p], n_p)
        mods_s = _split_mods(mod[n_p:n_c], n_s)

        ckv_p, kr_p, kcat_p, kvt_p, qcat_p, qg_p, kg_p, vg_p, lf_p = input_projections(
            yp, mods_p[1], mods_p[0], cos_p, sin_p, pw)
        olat_p = mla_prompt_attention(qcat_p, kcat_p, kvt_p, n_p, s_p)
        s0_p = jnp.zeros((n_p, H_A, DK_A, DV_A), F32)
        ogla_p, sfin_p = gla(qg_p, kg_p, vg_p, lf_p, s0_p, n_p, s_p, min(GLA_CHUNK_PROMPT, s_p))
        cnt0 = jnp.zeros((N_EXPERTS, LANES), F32)
        x1_p, h2_p, ids_p, wts_p, rank_p, cnt_p = merge_and_route(
            yp, [mods_p[1], mods_p[0], mods_p[2], mods_p[4], mods_p[3]], olat_p, ogla_p, mw, cnt0, alpha)

        ckv_s, kr_s, kcat_s, _, qcat_s, qg_s, kg_s, vg_s, lf_s = input_projections(
            ys, mods_s[1], mods_s[0], cos_s, sin_s, pw)
        qtok_s = jnp.transpose(qcat_s, (0, 2, 1, 3)).reshape(t_s, H_B * QCAT)
        otok_s = mla_sample_attention(qtok_s, kcat_s, cache_kv_latent, cache_krt, page_table, layer, n_s, s_s)
        olat_s = jnp.transpose(otok_s.reshape(t_s // TOKEN_TILE, TOKEN_TILE, H_B, KV_LORA), (0, 2, 1, 3))
        ogla_s, sfin_s = gla(qg_s, kg_s, vg_s, lf_s, state_gla[layer], n_s, s_s, s_s)
        x1_s, h2_s, ids_s, wts_s, rank_s, cnt = merge_and_route(
            ys, [mods_s[1], mods_s[0], mods_s[2], mods_s[4], mods_s[3]], olat_s, ogla_s, mw, cnt_p, alpha)

        h2 = jnp.concatenate([h2_p, h2_s], axis=0)
        ids = jnp.concatenate([ids_p, ids_s], axis=1)
        rank = jnp.concatenate([rank_p, rank_s], axis=1)
        tok_sorted, pos, pairs = dispatch(ids, rank, cnt)
        x_sorted = h2.at[tok_sorted].get(mode="promise_in_bounds")
        y_sorted = routed_expert_blocks(x_sorted, pairs, w["w_e_gate"], w["w_e_up"], w["w_e_down"])
        yg_p = y_sorted.at[pos[:, :t_p]].get(mode="promise_in_bounds")
        yg_s = y_sorted.at[pos[:, t_p:]].get(mode="promise_in_bounds")

        yp = combine_norm(x1_p, mods_p[5], shared_expert(h2_p, fw), yg_p, jnp.transpose(wts_p), fw, alpha)
        ys = combine_norm(x1_s, mods_s[5], shared_expert(h2_s, fw), yg_s, jnp.transpose(wts_s), fw, alpha)

        for lst, val in zip(outs, [ckv_p.reshape(n_p, s_p, KV_LORA), kr_p.reshape(n_p, s_p, QK_ROPE), sfin_p,
                                   ckv_s.reshape(n_s, s_s, KV_LORA), kr_s.reshape(n_s, s_s, QK_ROPE), sfin_s]):
            lst.append(val.astype(state_gla.dtype) if val.ndim == 4 else val)
    return (yp, ys) + tuple(jnp.stack(o) for o in outs)
```

```python
import functools
import math

import numpy as np
import jax
import jax.numpy as jnp
from jax import lax
from jax.experimental import pallas as pl
from jax.experimental.pallas import tpu as pltpu

F32 = jnp.float32
BF16 = jnp.bfloat16

D_MODEL = 1024
H_A, DK_A, DV_A, GLA_LR, GLA_TAU = 4, 128, 256, 16, 16.0
H_B, Q_LORA, KV_LORA, QK_NOPE, QK_ROPE, V_HEAD = 8, 384, 256, 128, 64, 128
ROPE_THETA = 10000.0
ATTN_SCALE = (QK_NOPE + QK_ROPE) ** -0.5
Q_SCALE = ATTN_SCALE * 1.4426950408889634
N_EXPERTS, TOP_K, N_GROUPS, TOPK_GROUPS = 256, 8, 8, 4
GROUP_SIZE = N_EXPERTS // N_GROUPS
D_EXPERT, D_SHARED, ROUTED_SCALE = 256, 256, 2.5
EPS = 1e-6
IN_SIZES = (Q_LORA, KV_LORA, QK_ROPE, H_A * DK_A, H_A * DK_A, H_A * DV_A, H_A * DV_A, GLA_LR, D_MODEL, D_MODEL)

LANES = 128
QCAT = KV_LORA + LANES
VMEM_LIMIT = 56 * 1024 * 1024

TOKEN_TILE = 512
FLASH_TQ, FLASH_TK = 512, 512
GLA_CHUNK_PROMPT = 256
MOE_ROWS = 512

NN = (((1,), (0,)), ((), ()))
NT = (((1,), (1,)), ((), ()))
TN = (((0,), (0,)), ((), ()))


def _dot(a, b, dims=NN):
    return lax.dot_general(a, b, dims, preferred_element_type=F32)


def _split3(x):
    x1 = x.astype(BF16)
    r1 = x - x1.astype(F32)
    x2 = r1.astype(BF16)
    x3 = (r1 - x2.astype(F32)).astype(BF16)
    return x1, x2, x3


def _dot_hi(a, b, dims=NN):
    a1, a2, a3 = _split3(a)
    b1, b2, b3 = _split3(b)
    small = _dot(a3, b1, dims) + _dot(a2, b2, dims) + _dot(a1, b3, dims)
    mid = _dot(a2, b1, dims) + _dot(a1, b2, dims)
    return (small + mid) + _dot(a1, b1, dims)


def _dot_hi3(a, b, dims=NN):
    a1 = a.astype(BF16)
    a2 = (a - a1.astype(F32)).astype(BF16)
    b1 = b.astype(BF16)
    b2 = (b - b1.astype(F32)).astype(BF16)
    return (_dot(a2, b1, dims) + _dot(a1, b2, dims)) + _dot(a1, b1, dims)


def _dot_hi_exact_lhs(w, x, dims=NN):
    x1, x2, x3 = _split3(x)
    return (_dot(w, x3, dims) + _dot(w, x2, dims)) + _dot(w, x1, dims)


def _sigmoid(x):
    return 1.0 / (1.0 + jnp.exp(-x))


def _silu(x):
    return x * _sigmoid(x)


def _rms(x, g):
    return x * lax.rsqrt(jnp.mean(x * x, axis=-1, keepdims=True) + EPS) * g


def _layer_norm(x, g, b):
    mu = jnp.mean(x, axis=-1, keepdims=True)
    xc = x - mu
    var = jnp.mean(xc * xc, axis=-1, keepdims=True)
    return xc * lax.rsqrt(var + EPS) * g + b


def _params(sem):
    return pltpu.CompilerParams(dimension_semantics=sem, vmem_limit_bytes=VMEM_LIMIT)


def _const_spec(shape):
    nd = len(shape)
    return pl.BlockSpec(shape, lambda *_: (0,) * nd)


def _adaln_kernel(c_ref, w_ref, b_ref, o_ref):
    o_ref[...] = _dot_hi(_silu(c_ref[...]), w_ref[...]) + b_ref[...]


def adaln(c, w_ada, b_ada):
    n, d = c.shape
    e = w_ada.shape[1]
    tn = 512
    return pl.pallas_call(
        _adaln_kernel,
        grid=(e // tn,),
        in_specs=[_const_spec((n, d)), pl.BlockSpec((d, tn), lambda j: (0, j)), pl.BlockSpec((1, tn), lambda j: (0, j))],
        out_specs=pl.BlockSpec((n, tn), lambda j: (0, j)),
        out_shape=jax.ShapeDtypeStruct((n, e), F32),
        compiler_params=_params(("arbitrary",)),
        name="adaln",
    )(c, w_ada, b_ada.reshape(1, e))


W1_COLS = Q_LORA + KV_LORA + 2 * LANES
WQ_COLS = H_B * QK_NOPE + 2 * H_B * LANES
WG_COLS = 2 * H_A * DK_A + H_A * DV_A + LANES


def _proj_kernel(x_ref, sc_ref, sh_ref, cos_ref, sin_ref, w1_ref, wq_ref, wuk_ref, wg_ref, wf2_ref, bf_ref,
                 gq_ref, gkv_ref,
                 ckv_ref, krope_ref, kcat_ref, kvt_ref, qcat_ref, qg_ref, kg_ref, vg_ref, logf_ref):
    gb, rb, d = x_ref.shape
    tm = gb * rb
    h = (x_ref[...] * (1.0 + sc_ref[...]) + sh_ref[...]).reshape(tm, d).astype(BF16)
    cos = cos_ref[...]
    sin = sin_ref[...]

    z1 = _dot(h, w1_ref[...])
    ckv = _rms(z1[:, Q_LORA:Q_LORA + KV_LORA], gkv_ref[...])
    o_kr = Q_LORA + KV_LORA
    krope = z1[:, o_kr:o_kr + LANES] * cos + z1[:, o_kr + LANES:o_kr + 2 * LANES] * sin
    ckv_ref[...] = ckv
    krope_ref[...] = krope[:, :QK_ROPE]
    kcat_ref[:, :KV_LORA] = ckv.astype(BF16)
    kcat_ref[:, KV_LORA:] = krope.astype(BF16)
    kvt_ref[...] = jnp.transpose(ckv).astype(BF16)

    qn = _rms(z1[:, :Q_LORA], gq_ref[...]).astype(BF16)
    q2 = _dot(qn, wq_ref[...])
    o_r = H_B * QK_NOPE
    o_s = o_r + H_B * LANES
    for hh in range(H_B):
        q_nope = q2[:, hh * QK_NOPE:(hh + 1) * QK_NOPE].astype(BF16)
        q_lat = _dot(q_nope, wuk_ref[hh])
        q_rope = q2[:, o_r + hh * LANES:o_r + (hh + 1) * LANES] * cos + q2[:, o_s + hh * LANES:o_s + (hh + 1) * LANES] * sin
        qcat_ref[0, hh, :, :KV_LORA] = (q_lat * Q_SCALE).astype(BF16)
        qcat_ref[0, hh, :, KV_LORA:] = (q_rope * Q_SCALE).astype(BF16)

    z2 = _dot(h, wg_ref[...])
    nk = H_A * DK_A
    qg_ref[...] = z2[:, :nk] * DK_A ** -0.5
    kg_ref[...] = z2[:, nk:2 * nk]
    vg_ref[...] = z2[:, 2 * nk:2 * nk + H_A * DV_A].astype(BF16)
    fa = z2[:, 2 * nk + H_A * DV_A:].astype(BF16)
    f_pre = _dot(fa, wf2_ref[...]) + bf_ref[...]
    log_sig = jnp.minimum(f_pre, 0.0) - jnp.log(1.0 + jnp.exp(-jnp.abs(f_pre)))
    logf_ref[...] = log_sig / GLA_TAU


def _group_tiling(g, r):
    if r >= TOKEN_TILE:
        assert r % TOKEN_TILE == 0
        return 1, TOKEN_TILE
    assert TOKEN_TILE % r == 0 and g % (TOKEN_TILE // r) == 0 and r % 8 == 0
    return TOKEN_TILE // r, r


def _group_specs(g, r, d):
    gb, rb = _group_tiling(g, r)
    nr = r // rb
    x_spec = pl.BlockSpec((gb, rb, d), lambda i: (i // nr, i % nr, 0))
    mod_spec = pl.BlockSpec((gb, 1, d), lambda i: (i // nr, 0, 0))
    return x_spec, mod_spec, (g // gb) * nr


def _tok_spec(cols):
    return pl.BlockSpec((TOKEN_TILE, cols), lambda i: (i, 0))


def input_projections(x, sc, sh, cos_tbl, sin_tbl, pw):
    g, r, d = x.shape
    t = g * r
    x_spec, mod_spec, n_tiles = _group_specs(g, r, d)
    n_tbl = cos_tbl.shape[0] // TOKEN_TILE
    tbl_spec = pl.BlockSpec((TOKEN_TILE, LANES), lambda i: (i % n_tbl, 0))
    nk = H_A * DK_A
    head_cols = [(KV_LORA, F32), (QK_ROPE, F32), (QCAT, BF16)]
    tail_cols = [(nk, F32), (nk, F32), (H_A * DV_A, BF16), (nk, F32)]
    mid_specs = [pl.BlockSpec((KV_LORA, TOKEN_TILE), lambda i: (0, i)),
                 pl.BlockSpec((1, H_B, TOKEN_TILE, QCAT), lambda i: (i, 0, 0, 0))]
    mid_shapes = [jax.ShapeDtypeStruct((KV_LORA, t), BF16), jax.ShapeDtypeStruct((n_tiles, H_B, TOKEN_TILE, QCAT), BF16)]
    return pl.pallas_call(
        _proj_kernel,
        grid=(n_tiles,),
        in_specs=[x_spec, mod_spec, mod_spec, tbl_spec, tbl_spec,
                  _const_spec(pw["w1"].shape), _const_spec(pw["wq"].shape), _const_spec(pw["wuk"].shape),
                  _const_spec(pw["wg"].shape), _const_spec(pw["wf2"].shape), _const_spec(pw["bf"].shape),
                  _const_spec(pw["gq"].shape), _const_spec(pw["gkv"].shape)],
        out_specs=[_tok_spec(c) for c, _ in head_cols] + mid_specs + [_tok_spec(c) for c, _ in tail_cols],
        out_shape=[jax.ShapeDtypeStruct((t, c), dt) for c, dt in head_cols] + mid_shapes
                  + [jax.ShapeDtypeStruct((t, c), dt) for c, dt in tail_cols],
        compiler_params=_params(("arbitrary",)),
        name="input_proj",
    )(x, sc, sh, cos_tbl, sin_tbl, pw["w1"], pw["wq"], pw["wuk"], pw["wg"], pw["wf2"], pw["bf"], pw["gq"], pw["gkv"])


def _flash_kernel(qi_ref, ki_ref, q_ref, k_ref, vt_ref, o_ref, m_sc, l_sc, acc_sc, *, tq, tk):
    step = pl.program_id(1)
    qi = qi_ref[step]
    ki = ki_ref[step]
    n_tiles, _, tt, _ = q_ref.shape
    cols = n_tiles * H_B * tt

    @pl.when(ki == 0)
    def _():
        m_sc[...] = jnp.full(m_sc.shape, -jnp.inf, F32)
        l_sc[...] = jnp.zeros(l_sc.shape, F32)
        acc_sc[...] = jnp.zeros(acc_sc.shape, F32)

    def update(masked):
        st = _dot(k_ref[...], q_ref[...].reshape(cols, QCAT), NT)
        if masked:
            key = ki * tk + lax.broadcasted_iota(jnp.int32, (tk, cols), 0)
            col = lax.broadcasted_iota(jnp.int32, (tk, cols), 1)
            tok = qi * tq + (col // (H_B * tt)) * tt + col % tt
            st = jnp.where(key <= tok, st, -jnp.inf)
        m_prev = m_sc[...]
        m_new = jnp.maximum(m_prev, jnp.max(st, axis=0, keepdims=True))
        corr = jnp.exp2(m_prev - m_new)
        p = jnp.exp2(st - m_new)
        l_sc[...] = l_sc[...] * corr + jnp.sum(p, axis=0, keepdims=True)
        acc_sc[...] = acc_sc[...] * corr + _dot(vt_ref[...], p.astype(BF16))
        m_sc[...] = m_new

    crosses_diagonal = (ki + 1) * tk - 1 > qi * tq

    @pl.when(crosses_diagonal)
    def _():
        update(True)

    @pl.when(jnp.logical_not(crosses_diagonal))
    def _():
        update(False)

    @pl.when((ki + 1) * tk >= (qi + 1) * tq)
    def _():
        for u in range(n_tiles):
            for hh in range(H_B):
                cs = slice((u * H_B + hh) * tt, (u * H_B + hh + 1) * tt)
                o_ref[u, hh] = jnp.transpose(acc_sc[:, cs] / l_sc[:, cs]).astype(o_ref.dtype)


def mla_prompt_attention(qcat, kcat, kvt, n_seq, seq):
    tt = qcat.shape[2]
    n_tiles = max(1, min(FLASH_TQ, seq) // tt)
    tq, tk = n_tiles * tt, min(FLASH_TK, seq)
    assert seq % tq == 0 and seq % tk == 0
    nq, nkb = seq // tq, seq // tk
    rows = tq * H_B
    qi_list, ki_list = [], []
    for qi in range(nq):
        for ki in range(-(-((qi + 1) * tq) // tk)):
            qi_list.append(qi)
            ki_list.append(ki)
    qi_tbl = jnp.asarray(np.array(qi_list, np.int32))
    ki_tbl = jnp.asarray(np.array(ki_list, np.int32))
    grid_spec = pltpu.PrefetchScalarGridSpec(
        num_scalar_prefetch=2,
        grid=(n_seq, len(qi_list)),
        in_specs=[pl.BlockSpec((n_tiles, H_B, tt, QCAT), lambda b, s, qt, kt: (b * nq + qt[s], 0, 0, 0)),
                  pl.BlockSpec((tk, QCAT), lambda b, s, qt, kt: (b * nkb + kt[s], 0)),
                  pl.BlockSpec((KV_LORA, tk), lambda b, s, qt, kt: (0, b * nkb + kt[s]))],
        out_specs=pl.BlockSpec((n_tiles, H_B, tt, KV_LORA), lambda b, s, qt, kt: (b * nq + qt[s], 0, 0, 0)),
        scratch_shapes=[pltpu.VMEM((1, rows), F32), pltpu.VMEM((1, rows), F32), pltpu.VMEM((KV_LORA, rows), F32)],
    )
    return pl.pallas_call(
        functools.partial(_flash_kernel, tq=tq, tk=tk),
        grid_spec=grid_spec,
        out_shape=jax.ShapeDtypeStruct(qcat.shape[:3] + (KV_LORA,), BF16),
        compiler_params=_params(("arbitrary", "arbitrary")),
        name="mla_prompt_attn",
    )(qi_tbl, ki_tbl, qcat, kcat, kvt)


PAGES_PER_CHUNK = 64
PAGED_STREAMS = 4


def _paged_kernel(pt_ref, q_ref, knew_ref, ckv_hbm, ckrt_hbm, o_ref, kv_buf, krt_buf, sems, *,
                  n_new, layer, page, chunk_pages, n_chunks, n_str):
    b = pl.program_id(0)
    n_b = pl.num_programs(0)
    rows = n_new * H_B

    def page_copies(seq, chunk, slot, j):
        phys = pt_ref[seq, chunk * chunk_pages + j]
        dst = pl.ds(j * page, page)
        dst_t = pl.ds(j * QK_ROPE, QK_ROPE)
        return (pltpu.make_async_copy(ckv_hbm.at[layer, phys], kv_buf.at[slot, dst], sems.at[0, slot]),
                pltpu.make_async_copy(ckrt_hbm.at[layer, phys], krt_buf.at[slot, dst_t], sems.at[1, slot]))

    def start_chunk(seq, chunk, slot):
        for j in range(chunk_pages):
            for cp in page_copies(seq, chunk, slot, j):
                cp.start()

    def wait_chunk(seq, chunk, slot):
        for j in range(chunk_pages):
            for cp in page_copies(seq, chunk, slot, j):
                cp.wait()

    @pl.when(b == 0)
    def _():
        start_chunk(0, 0, 0)

    q = q_ref[0]
    qf = q.astype(F32)
    kn = knew_ref[0].astype(F32)
    tok = lax.broadcasted_iota(jnp.int32, (rows, 1), 0) // H_B
    cols = []
    for j in range(n_new):
        sj = jnp.sum(qf * kn[j:j + 1, :], axis=-1, keepdims=True)
        cols.append(jnp.where(j <= tok, sj, -jnp.inf))
    m = cols[0]
    for j in range(1, n_new):
        m = jnp.maximum(m, cols[j])
    l = jnp.zeros((rows, 1), F32)
    acc = jnp.zeros((rows, KV_LORA), F32)
    for j in range(n_new):
        pj = jnp.exp2(cols[j] - m)
        l = l + pj
        acc = acc + pj * kn[j:j + 1, :KV_LORA]

    q_lat = q[:, :KV_LORA]
    q_rope = q[:, KV_LORA:KV_LORA + QK_ROPE]
    sp = chunk_pages // n_str
    state = [(m, l, acc)] + [(jnp.full((rows, 1), -jnp.inf, F32), jnp.zeros((rows, 1), F32),
                              jnp.zeros((rows, KV_LORA), F32)) for _ in range(n_str - 1)]
    for c in range(n_chunks):
        slot = c % 2
        if c + 1 < n_chunks:
            start_chunk(b, c + 1, 1 - slot)
        else:
            @pl.when(b + 1 < n_b)
            def _():
                start_chunk(b + 1, 0, 1 - slot)
        wait_chunk(b, c, slot)
        for st in range(n_str):
            m, l, acc = state[st]
            kv = kv_buf[slot, st * sp * page:(st + 1) * sp * page, :].astype(BF16)
            s_rope = [_dot(q_rope, krt_buf[slot, j * QK_ROPE:(j + 1) * QK_ROPE, :].astype(BF16))
                      for j in range(st * sp, (st + 1) * sp)]
            s = _dot(q_lat, kv, NT) + jnp.concatenate(s_rope, axis=-1)
            m_new = jnp.maximum(m, jnp.max(s, axis=-1, keepdims=True))
            corr = jnp.exp2(m - m_new)
            p = jnp.exp2(s - m_new)
            l = l * corr + jnp.sum(p, axis=-1, keepdims=True)
            acc = acc * corr + _dot(p.astype(BF16), kv)
            state[st] = (m_new, l, acc)
    m = state[0][0]
    for st in range(1, n_str):
        m = jnp.maximum(m, state[st][0])
    l = jnp.zeros((rows, 1), F32)
    acc = jnp.zeros((rows, KV_LORA), F32)
    for m_s, l_s, acc_s in state:
        w = jnp.exp2(m_s - m)
        l = l + l_s * w
        acc = acc + acc_s * w
    o_ref[0] = (acc / l).astype(o_ref.dtype)


def mla_sample_attention(qcat, kcat, cache_kv, cache_krt, page_table, layer, n_seq, n_new):
    n_pages = page_table.shape[1]
    page = cache_kv.shape[2]
    rows = n_new * H_B
    chunk_pages = min(PAGES_PER_CHUNK, n_pages // 2)
    assert n_pages % (2 * chunk_pages) == 0
    n_streams = math.gcd(PAGED_STREAMS, chunk_pages)
    n_chunks = n_pages // chunk_pages
    q3 = qcat.reshape(n_seq, rows, QCAT)
    k3 = kcat.reshape(n_seq, n_new, QCAT)
    grid_spec = pltpu.PrefetchScalarGridSpec(
        num_scalar_prefetch=1,
        grid=(n_seq,),
        in_specs=[pl.BlockSpec((1, rows, QCAT), lambda b, pt: (b, 0, 0)),
                  pl.BlockSpec((1, n_new, QCAT), lambda b, pt: (b, 0, 0)),
                  pl.BlockSpec(memory_space=pl.ANY),
                  pl.BlockSpec(memory_space=pl.ANY)],
        out_specs=pl.BlockSpec((1, rows, KV_LORA), lambda b, pt: (b, 0, 0)),
        scratch_shapes=[pltpu.VMEM((2, chunk_pages * page, KV_LORA), F32),
                        pltpu.VMEM((2, chunk_pages * QK_ROPE, page), F32),
                        pltpu.SemaphoreType.DMA((2, 2))],
    )
    o = pl.pallas_call(
        functools.partial(_paged_kernel, n_new=n_new, layer=layer, page=page, chunk_pages=chunk_pages,
                          n_chunks=n_chunks, n_str=n_streams),
        grid_spec=grid_spec,
        out_shape=jax.ShapeDtypeStruct((n_seq, rows, KV_LORA), BF16),
        compiler_params=_params(("arbitrary",)),
        name="mla_sample_attn",
    )(page_table, q3, k3, cache_kv, cache_krt)
    return o.reshape(n_seq * n_new, H_B * KV_LORA)


def _gla_cumsum_matrix(chunk):
    blocks = [np.tril(np.ones((chunk, chunk), np.float32))]
    m = chunk // 2
    while m >= 1:
        w = np.zeros((chunk, chunk), np.float32)
        for t in range(chunk):
            seg = (t // m) * m
            if (t % (2 * m)) >= m:
                w[t, seg:t + 1] = 1.0
            else:
                w[t, t + 1:seg + m] = 1.0
        blocks.append(w)
        m //= 2
    return np.concatenate(blocks, axis=0)


def _gla_kernel(q_ref, k_ref, v_ref, g_ref, wc_ref, s0_ref, o_ref, sfin_ref, s_sc, *, chunk):
    c_idx = pl.program_id(1)

    @pl.when(c_idx == 0)
    def _():
        s_sc[...] = s0_ref[0]

    n_lev = chunk.bit_length() - 1
    row = lax.broadcasted_iota(jnp.int32, (chunk, chunk), 0)
    col = lax.broadcasted_iota(jnp.int32, (chunk, chunk), 1)
    row_in_chunk = lax.broadcasted_iota(jnp.int32, (chunk, DK_A), 0)
    sums_all = _dot_hi_exact_lhs(wc_ref[...], g_ref[...])
    for hh in range(H_A):
        q = q_ref[:, hh * DK_A:(hh + 1) * DK_A]
        k = k_ref[:, hh * DK_A:(hh + 1) * DK_A]
        v = v_ref[:, hh * DV_A:(hh + 1) * DV_A]
        sums = sums_all[:, hh * DK_A:(hh + 1) * DK_A]
        b = sums[:chunk]
        b_last = b[chunk - 1:chunk, :]
        s_prev = s_sc[hh]

        a = jnp.where(row == col, _dot(q.astype(BF16), k.astype(BF16), NT), 0.0)
        for lev in range(n_lev):
            m = chunk >> (lev + 1)
            scale = jnp.exp(sums[(lev + 1) * chunk:(lev + 2) * chunk])
            upper = (row_in_chunk % (2 * m)) >= m
            q_l = jnp.where(upper, q * scale, 0.0).astype(BF16)
            k_l = jnp.where(upper, 0.0, k * scale).astype(BF16)
            same_block = (row // (2 * m)) == (col // (2 * m))
            a = a + jnp.where(same_block, _dot(q_l, k_l, NT), 0.0)

        inter = _dot((q * jnp.exp(b)).astype(BF16), s_prev.astype(BF16))
        o_ref[:, hh * DV_A:(hh + 1) * DV_A] = inter + _dot(a.astype(BF16), v)

        k_dec = (k * jnp.exp(b_last - b)).astype(BF16)
        decay_col = jnp.transpose(jnp.broadcast_to(jnp.exp(b_last), (DK_A, DK_A)))[:, :1]
        s_sc[hh] = decay_col * s_prev + _dot(k_dec, v, TN)

    @pl.when(c_idx == pl.num_programs(1) - 1)
    def _():
        sfin_ref[0] = s_sc[...]


def gla(qg, kg, vg, logf, s0, n_seq, seq, chunk):
    assert seq % chunk == 0 and chunk & (chunk - 1) == 0 and chunk % 8 == 0
    n_chunks = seq // chunk
    wc = jnp.asarray(_gla_cumsum_matrix(chunk), BF16)
    nk = H_A * DK_A

    def tok(cols):
        return pl.BlockSpec((chunk, cols), lambda b, c: (b * n_chunks + c, 0))

    state_spec = pl.BlockSpec((1, H_A, DK_A, DV_A), lambda b, c: (b, 0, 0, 0))
    return pl.pallas_call(
        functools.partial(_gla_kernel, chunk=chunk),
        grid=(n_seq, n_chunks),
        in_specs=[tok(nk), tok(nk), tok(H_A * DV_A), tok(nk), _const_spec(wc.shape), state_spec],
        out_specs=[tok(H_A * DV_A), state_spec],
        out_shape=[jax.ShapeDtypeStruct((n_seq * seq, H_A * DV_A), F32),
                   jax.ShapeDtypeStruct((n_seq, H_A, DK_A, DV_A), F32)],
        scratch_shapes=[pltpu.VMEM((H_A, DK_A, DV_A), F32)],
        compiler_params=_params(("arbitrary", "arbitrary")),
        name="gla",
    )(qg, kg, vg, logf, wc, s0)


def _route(h2, wrt_ref, rbias_ref, tm):
    logits = _dot_hi3(wrt_ref[...], h2, NT)
    scores = _sigmoid(logits)
    sel = scores + rbias_ref[...]
    neg = -jnp.inf

    iota_g = lax.broadcasted_iota(jnp.int32, (GROUP_SIZE, tm), 0).astype(F32)
    g_rows = []
    for g in range(N_GROUPS):
        blk = sel[g * GROUP_SIZE:(g + 1) * GROUP_SIZE]
        m1 = jnp.max(blk, axis=0, keepdims=True)
        first = jnp.min(jnp.where(blk == m1, iota_g, float(GROUP_SIZE)), axis=0, keepdims=True)
        m2 = jnp.max(jnp.where(iota_g == first, neg, blk), axis=0, keepdims=True)
        g_rows.append(m1 + m2)
    g_score = jnp.concatenate(g_rows, axis=0)

    iota_n = lax.broadcasted_iota(jnp.int32, (N_GROUPS, tm), 0).astype(F32)
    g_keep = jnp.zeros((N_GROUPS, tm), F32)
    for _ in range(TOPK_GROUPS):
        mx = jnp.max(g_score, axis=0, keepdims=True)
        first = jnp.min(jnp.where(g_score == mx, iota_n, float(N_GROUPS)), axis=0, keepdims=True)
        hit = iota_n == first
        g_keep = jnp.where(hit, 1.0, g_keep)
        g_score = jnp.where(hit, neg, g_score)

    sel_m = jnp.concatenate(
        [jnp.where(g_keep[g:g + 1] > 0.0, sel[g * GROUP_SIZE:(g + 1) * GROUP_SIZE], neg) for g in range(N_GROUPS)], axis=0)

    iota_e = lax.broadcasted_iota(jnp.int32, (N_EXPERTS, tm), 0).astype(F32)
    ids, wts, hits = [], [], []
    for _ in range(TOP_K):
        mx = jnp.max(sel_m, axis=0, keepdims=True)
        first = jnp.min(jnp.where(sel_m == mx, iota_e, float(N_EXPERTS)), axis=0, keepdims=True)
        hit = iota_e == first
        ids.append(first)
        hits.append(hit)
        wts.append(jnp.sum(jnp.where(hit, scores, 0.0), axis=0, keepdims=True))
        sel_m = jnp.where(hit, neg, sel_m)
    ids = jnp.concatenate(ids, axis=0)
    wts = jnp.concatenate(wts, axis=0)
    wts = wts / jnp.sum(wts, axis=0, keepdims=True) * ROUTED_SCALE
    return ids, wts, hits


def _rank_in_expert(hits, cnt_sc, tm):
    chosen = jnp.where(hits[0], 1.0, 0.0)
    for hit in hits[1:]:
        chosen = jnp.where(hit, 1.0, chosen)
    earlier = (lax.broadcasted_iota(jnp.int32, (tm, tm), 0) < lax.broadcasted_iota(jnp.int32, (tm, tm), 1))
    prefix = _dot(chosen.astype(BF16), jnp.where(earlier, 1.0, 0.0).astype(BF16))
    base = cnt_sc[:, :1] + prefix
    ranks = [jnp.sum(jnp.where(hit, base, 0.0), axis=0, keepdims=True) for hit in hits]
    cnt_sc[...] = cnt_sc[...] + jnp.sum(chosen, axis=1, keepdims=True)
    return jnp.concatenate(ranks, axis=0)


def _merge_kernel(x_ref, scm_ref, shm_ref, gm_ref, scf_ref, shf_ref, olat_ref, ogla_ref,
                  w3_ref, wuv_ref, wb_ref, wa_ref, wo_ref, ggla_ref, ln1g_ref, ln1b_ref, wrt_ref, rbias_ref, cnt0_ref,
                  x1_ref, h2_ref, ids_ref, wts_ref, rank_ref, cnt_ref, cnt_sc, *, alpha):
    @pl.when(pl.program_id(0) == 0)
    def _():
        cnt_sc[...] = cnt0_ref[...]

    gb, rb, d = x_ref.shape
    tm = gb * rb
    x = x_ref[...]
    h = (x * (1.0 + scm_ref[...]) + shm_ref[...]).reshape(tm, d).astype(BF16)
    z3 = _dot(h, w3_ref[...])

    vb = [_dot(olat_ref[0, hh], wuv_ref[hh]).astype(BF16) for hh in range(H_B)]
    y_b = _dot(jnp.concatenate(vb, axis=-1), wb_ref[...])

    ga = []
    for hh in range(H_A):
        o_n = _rms(ogla_ref[:, hh * DV_A:(hh + 1) * DV_A], ggla_ref[...])
        ga.append((o_n * _silu(z3[:, hh * DV_A:(hh + 1) * DV_A])).astype(BF16))
    y_a = _dot(jnp.concatenate(ga, axis=-1), wa_ref[...])

    merged = _sigmoid(z3[:, d:2 * d]) * y_a + _sigmoid(z3[:, 2 * d:]) * y_b
    mix = _dot(merged.astype(BF16), wo_ref[...]).reshape(gb, rb, d)
    x1 = _layer_norm(alpha * x + gm_ref[...] * mix, ln1g_ref[...], ln1b_ref[...])
    x1_ref[...] = x1
    h2 = (x1 * (1.0 + scf_ref[...]) + shf_ref[...]).reshape(tm, d)
    h2_ref[...] = h2.astype(BF16)
    ids, wts, hits = _route(h2, wrt_ref, rbias_ref, tm)
    ids_ref[...] = ids.astype(jnp.int32)
    wts_ref[...] = wts
    rank_ref[...] = _rank_in_expert(hits, cnt_sc, tm).astype(jnp.int32)
    cnt_ref[...] = cnt_sc[...]


def merge_and_route(x, mods, olat, ogla, mw, cnt0, alpha):
    g, r, d = x.shape
    t = g * r
    x_spec, mod_spec, n_tiles = _group_specs(g, r, d)
    names = ["w3", "wuv", "wb", "wa", "wo", "ggla", "ln1g", "ln1b", "wrt", "rbias"]
    kt_spec = pl.BlockSpec((TOP_K, TOKEN_TILE), lambda i: (0, i))
    cnt_spec = _const_spec((N_EXPERTS, LANES))
    return pl.pallas_call(
        functools.partial(_merge_kernel, alpha=alpha),
        grid=(n_tiles,),
        in_specs=[x_spec] + [mod_spec] * 5
                 + [pl.BlockSpec((1, H_B, TOKEN_TILE, KV_LORA), lambda i: (i, 0, 0, 0)), _tok_spec(H_A * DV_A)]
                 + [_const_spec(mw[n].shape) for n in names] + [cnt_spec],
        out_specs=[x_spec, _tok_spec(d), kt_spec, kt_spec, kt_spec, cnt_spec],
        out_shape=[jax.ShapeDtypeStruct((g, r, d), F32), jax.ShapeDtypeStruct((t, d), BF16),
                   jax.ShapeDtypeStruct((TOP_K, t), jnp.int32), jax.ShapeDtypeStruct((TOP_K, t), F32),
                   jax.ShapeDtypeStruct((TOP_K, t), jnp.int32), jax.ShapeDtypeStruct((N_EXPERTS, LANES), F32)],
        scratch_shapes=[pltpu.VMEM((N_EXPERTS, LANES), F32)],
        compiler_params=_params(("arbitrary",)),
        name="merge_route",
    )(x, *mods, olat, ogla, *[mw[n] for n in names], cnt0)


def _moe_kernel(pb_ref, pe_ref, lo_ref, hi_ref, x_ref, wg_ref, wu_ref, wd_ref, y_ref, wg_sc, wu_sc, wd_sc):
    j = pl.program_id(0)
    prev = jnp.maximum(j - 1, 0)
    new_expert = jnp.logical_or(j == 0, pe_ref[j] != pe_ref[prev])
    first_visit = jnp.logical_or(j == 0, pb_ref[j] != pb_ref[prev])
    lo = lo_ref[j]
    hi = hi_ref[j]

    @pl.when(new_expert)
    def _():
        wg_sc[...] = wg_ref[0].astype(BF16)
        wu_sc[...] = wu_ref[0].astype(BF16)
        wd_sc[...] = wd_ref[0].astype(BF16)

    @pl.when(first_visit)
    def _():
        y_ref[...] = jnp.zeros(y_ref.shape, y_ref.dtype)

    @pl.when(hi > lo)
    def _():
        x = x_ref[...]
        gate = _dot(x, wg_sc[...])
        up = _dot(x, wu_sc[...])
        act = (_silu(gate) * up).astype(BF16)
        y = _dot(act, wd_sc[...])
        row = lax.broadcasted_iota(jnp.int32, (y.shape[0], 1), 0)
        mine = jnp.logical_and(row >= lo, row < hi)
        y_ref[...] = jnp.where(mine, y, y_ref[...].astype(F32)).astype(y_ref.dtype)


def routed_expert_blocks(x_sorted, pairs, w_gate, w_up, w_down):
    m, d = x_sorted.shape
    de = w_gate.shape[-1]
    n_pairs = pairs[0].shape[0]
    grid_spec = pltpu.PrefetchScalarGridSpec(
        num_scalar_prefetch=4,
        grid=(n_pairs,),
        in_specs=[pl.BlockSpec((MOE_ROWS, d), lambda j, pb, pe, lo, hi: (pb[j], 0)),
                  pl.BlockSpec((1, d, de), lambda j, pb, pe, lo, hi: (pe[j], 0, 0)),
                  pl.BlockSpec((1, d, de), lambda j, pb, pe, lo, hi: (pe[j], 0, 0)),
                  pl.BlockSpec((1, de, d), lambda j, pb, pe, lo, hi: (pe[j], 0, 0))],
        out_specs=pl.BlockSpec((MOE_ROWS, d), lambda j, pb, pe, lo, hi: (pb[j], 0)),
        scratch_shapes=[pltpu.VMEM((d, de), BF16), pltpu.VMEM((d, de), BF16), pltpu.VMEM((de, d), BF16)],
    )
    return pl.pallas_call(
        _moe_kernel,
        grid_spec=grid_spec,
        out_shape=jax.ShapeDtypeStruct((m, d), BF16),
        compiler_params=_params(("arbitrary",)),
        name="moe_experts",
    )(*pairs, x_sorted, w_gate, w_up, w_down)


def _pos_kernel(ids_ref, rank_ref, pstart_ref, pos_ref):
    k, tm = ids_ref.shape
    iota_e = lax.broadcasted_iota(jnp.int32, (N_EXPERTS, tm), 0)
    pstart = pstart_ref[:, :1]
    base = [jnp.sum(jnp.where(iota_e == ids_ref[j:j + 1, :], pstart, 0.0), axis=0, keepdims=True) for j in range(k)]
    pos_ref[...] = jnp.concatenate(base, axis=0).astype(jnp.int32) + rank_ref[...]


def dispatch(ids, rank, counts):
    k, t = ids.shape
    m = k * t
    assert m % MOE_ROWS == 0
    n_blk = m // MOE_ROWS
    counts = counts[:, 0].astype(jnp.int32)
    uend = jnp.cumsum(counts)
    ustart = uend - counts
    pstart = jnp.broadcast_to(ustart.astype(F32)[:, None], (N_EXPERTS, LANES))
    kt_spec = pl.BlockSpec((k, TOKEN_TILE), lambda i: (0, i))
    pos = pl.pallas_call(
        _pos_kernel,
        grid=(t // TOKEN_TILE,),
        in_specs=[kt_spec, kt_spec, _const_spec((N_EXPERTS, LANES))],
        out_specs=kt_spec,
        out_shape=jax.ShapeDtypeStruct((k, t), jnp.int32),
        compiler_params=_params(("arbitrary",)),
        name="dispatch_positions",
    )(ids, rank, pstart)
    tok = jnp.broadcast_to(jnp.arange(t, dtype=jnp.int32)[None, :], (k, t))
    _, tok_sorted = lax.sort_key_val(pos.reshape(m), tok.reshape(m))

    first_blk = ustart // MOE_ROWS
    n_pairs_e = jnp.where(counts > 0, (uend - 1) // MOE_ROWS - first_blk + 1, 0)
    pair_end = jnp.cumsum(n_pairs_e)
    pair_start = pair_end - n_pairs_e
    total = pair_end[-1]
    j = jnp.arange(n_blk + N_EXPERTS, dtype=jnp.int32)
    valid = j < total
    e_j = jnp.minimum(jnp.searchsorted(pair_end, j, side="right"), N_EXPERTS - 1).astype(jnp.int32)
    e_j = jnp.where(valid, e_j, e_j[total - 1])
    b_j = jnp.where(valid, first_blk[e_j] + j - pair_start[e_j], n_blk - 1).astype(jnp.int32)
    lo = jnp.where(valid, jnp.maximum(ustart[e_j], b_j * MOE_ROWS) - b_j * MOE_ROWS, 0).astype(jnp.int32)
    hi = jnp.where(valid, jnp.minimum(uend[e_j], (b_j + 1) * MOE_ROWS) - b_j * MOE_ROWS, 0).astype(jnp.int32)
    return tok_sorted, pos, (b_j, e_j, lo, hi)


def _final_kernel(x1_ref, gf_ref, h2_ref, yg_ref, wts_ref, wsg_ref, wsu_ref, wsd_ref, ln2g_ref, ln2b_ref, y_ref, *, alpha):
    gb, rb, d = x1_ref.shape
    h2 = h2_ref[...]
    act = (_silu(_dot(h2, wsg_ref[...])) * _dot(h2, wsu_ref[...])).astype(BF16)
    ffn = _dot(act, wsd_ref[...])
    w = wts_ref[...]
    routed = yg_ref[0].astype(F32) * w[:, 0:1]
    for j in range(1, TOP_K):
        routed = routed + yg_ref[j].astype(F32) * w[:, j:j + 1]
    ffn = (ffn + routed).reshape(gb, rb, d)
    y_ref[...] = _layer_norm(alpha * x1_ref[...] + gf_ref[...] * ffn, ln2g_ref[...], ln2b_ref[...])


def shared_combine_norm(x1, gf, h2, yg, wts_t, fw, alpha):
    g, r, d = x1.shape
    x_spec, mod_spec, n_tiles = _group_specs(g, r, d)
    names = ["wsg", "wsu", "wsd", "ln2g", "ln2b"]
    return pl.pallas_call(
        functools.partial(_final_kernel, alpha=alpha),
        grid=(n_tiles,),
        in_specs=[x_spec, mod_spec, _tok_spec(d), pl.BlockSpec((TOP_K, TOKEN_TILE, d), lambda i: (0, i, 0)),
                  _tok_spec(TOP_K)] + [_const_spec(fw[n].shape) for n in names],
        out_specs=x_spec,
        out_shape=jax.ShapeDtypeStruct((g, r, d), F32),
        compiler_params=_params(("arbitrary",)),
        name="shared_combine_norm",
    )(x1, gf, h2, yg, wts_t, *[fw[n] for n in names])


def _rot_cols(w):
    half = w.shape[-1] // 2
    return jnp.concatenate([-w[..., half:], w[..., :half]], axis=-1)


def _pad_cols(w, width):
    return jnp.pad(w, [(0, 0)] * (w.ndim - 1) + [(0, width - w.shape[-1])])


def _prepare_weights(w):
    cuts = np.cumsum((0,) + IN_SIZES)
    part = [w["w_in"][:, cuts[i]:cuts[i + 1]] for i in range(len(IN_SIZES))]
    w_qa, w_kva, w_kr, w_gq, w_gk, w_gv, w_go, w_gf, w_ga, w_gb = part
    w1 = jnp.concatenate([w_qa, w_kva, _pad_cols(w_kr, LANES), _pad_cols(_rot_cols(w_kr), LANES)], axis=1)
    wuq = w["w_uq"].reshape(Q_LORA, H_B, QK_NOPE + QK_ROPE)
    wuq_rope = wuq[:, :, QK_NOPE:]
    wq = jnp.concatenate([wuq[:, :, :QK_NOPE].reshape(Q_LORA, H_B * QK_NOPE),
                          _pad_cols(wuq_rope, LANES).reshape(Q_LORA, H_B * LANES),
                          _pad_cols(_rot_cols(wuq_rope), LANES).reshape(Q_LORA, H_B * LANES)], axis=1)
    wg = jnp.concatenate([w_gq, w_gk, w_gv, _pad_cols(w_gf, LANES)], axis=1)
    proj = dict(
        w1=w1.astype(BF16), wq=wq.astype(BF16), wg=wg.astype(BF16),
        wuk=jnp.transpose(w["w_uk"], (1, 2, 0)).astype(BF16),
        wf2=jnp.pad(w["w_gla_f2"], ((0, LANES - GLA_LR), (0, 0))).astype(BF16),
        bf=w["b_gla_f"].reshape(1, -1), gq=w["g_q_norm"].reshape(1, -1), gkv=w["g_kv_norm"].reshape(1, -1))
    merge = dict(
        w3=jnp.concatenate([w_go, w_ga, w_gb], axis=1).astype(BF16),
        wuv=jnp.transpose(w["w_uv"], (1, 0, 2)).astype(BF16),
        wb=w["w_b_out"].astype(BF16), wa=w["w_a_out"].astype(BF16), wo=w["w_o"].astype(BF16),
        ggla=w["g_gla_norm"].reshape(1, -1), ln1g=w["ln1_g"].reshape(1, -1), ln1b=w["ln1_b"].reshape(1, -1),
        wrt=jnp.transpose(w["w_router"]), rbias=w["router_bias"].reshape(-1, 1))
    final = dict(
        wsg=w["w_s_gate"].astype(BF16), wsu=w["w_s_up"].astype(BF16), wsd=w["w_s_down"].astype(BF16),
        ln2g=w["ln2_g"].reshape(1, -1), ln2b=w["ln2_b"].reshape(1, -1))
    return proj, merge, final


def _rope_tables(pos, tile_rows):
    half = QK_ROPE // 2
    freqs = ROPE_THETA ** (-jnp.arange(half, dtype=F32) / half)
    ang = pos.astype(F32)[:, None] * freqs
    cos = _pad_cols(jnp.concatenate([jnp.cos(ang)] * 2, axis=-1), LANES)
    sin = _pad_cols(jnp.concatenate([jnp.sin(ang)] * 2, axis=-1), LANES)
    if pos.shape[0] < tile_rows:
        rep = tile_rows // pos.shape[0]
        cos, sin = jnp.tile(cos, (rep, 1)), jnp.tile(sin, (rep, 1))
    return cos, sin


def _split_mods(mod, n):
    return [m.reshape(n, 1, D_MODEL) for m in jnp.split(mod, 6, axis=-1)]


def kernel(x_prompt, x_sample, c_prompt, c_sample, cache_kv_latent, cache_k_rope, state_gla, page_table, w_ada, b_ada, w_in, g_q_norm, w_uq, g_kv_norm, w_uk, w_uv, w_gla_f2, b_gla_f, g_gla_norm, w_a_out, w_b_out, w_o, ln1_g, ln1_b, w_router, router_bias, w_e_gate, w_e_up, w_e_down, w_s_gate, w_s_up, w_s_down, ln2_g, ln2_b):
    w_all = dict(w_ada=w_ada, b_ada=b_ada, w_in=w_in, g_q_norm=g_q_norm, w_uq=w_uq, g_kv_norm=g_kv_norm, w_uk=w_uk,
                 w_uv=w_uv, w_gla_f2=w_gla_f2, b_gla_f=b_gla_f, g_gla_norm=g_gla_norm, w_a_out=w_a_out,
                 w_b_out=w_b_out, w_o=w_o, ln1_g=ln1_g, ln1_b=ln1_b, w_router=w_router, router_bias=router_bias,
                 w_e_gate=w_e_gate, w_e_up=w_e_up, w_e_down=w_e_down, w_s_gate=w_s_gate, w_s_up=w_s_up,
                 w_s_down=w_s_down, ln2_g=ln2_g, ln2_b=ln2_b)
    depth = w_ada.shape[0]
    alpha = (2.0 * depth) ** 0.25
    n_p, s_p, _ = x_prompt.shape
    n_s, s_s, _ = x_sample.shape
    t_p, t_s = n_p * s_p, n_s * s_s
    past_len = page_table.shape[1] * cache_kv_latent.shape[2]
    cache_krt = jnp.swapaxes(cache_k_rope, 2, 3)
    cos_p, sin_p = _rope_tables(jnp.arange(s_p, dtype=jnp.int32), TOKEN_TILE)
    cos_s, sin_s = _rope_tables(past_len + jnp.arange(s_s, dtype=jnp.int32), TOKEN_TILE)

    n_c = n_p + n_s
    n_c_pad = -(-n_c // 16) * 16
    c_all = jnp.pad(jnp.concatenate([c_prompt, c_sample], axis=0), ((0, n_c_pad - n_c), (0, 0)))

    yp, ys = x_prompt, x_sample
    outs = [[] for _ in range(6)]
    for layer in range(depth):
        w = {name: arr[layer] for name, arr in w_all.items()}
        pw, mw, fw = _prepare_weights(w)
        mod = adaln(c_all, w["w_ada"], w["b_ada"])
        mods_p = _split_mods(mod[:n_p], n_p)
        mods_s = _split_mods(mod[n_p:n_c], n_s)

        ckv_p, kr_p, kcat_p, kvt_p, qcat_p, qg_p, kg_p, vg_p, lf_p = input_projections(
            yp, mods_p[1], mods_p[0], cos_p, sin_p, pw)
        olat_p = mla_prompt_attention(qcat_p, kcat_p, kvt_p, n_p, s_p)
        s0_p = jnp.zeros((n_p, H_A, DK_A, DV_A), F32)
        ogla_p, sfin_p = gla(qg_p, kg_p, vg_p, lf_p, s0_p, n_p, s_p, min(GLA_CHUNK_PROMPT, s_p))
        cnt0 = jnp.zeros((N_EXPERTS, LANES), F32)
        x1_p, h2_p, ids_p, wts_p, rank_p, cnt_p = merge_and_route(
            yp, [mods_p[1], mods_p[0], mods_p[2], mods_p[4], mods_p[3]], olat_p, ogla_p, mw, cnt0, alpha)

        ckv_s, kr_s, kcat_s, _, qcat_s, qg_s, kg_s, vg_s, lf_s = input_projections(
            ys, mods_s[1], mods_s[0], cos_s, sin_s, pw)
        qtok_s = jnp.transpose(qcat_s, (0, 2, 1, 3)).reshape(t_s, H_B * QCAT)
        otok_s = mla_sample_attention(qtok_s, kcat_s, cache_kv_latent, cache_krt, page_table, layer, n_s, s_s)
        olat_s = jnp.transpose(otok_s.reshape(t_s // TOKEN_TILE, TOKEN_TILE, H_B, KV_LORA), (0, 2, 1, 3))
        ogla_s, sfin_s = gla(qg_s, kg_s, vg_s, lf_s, state_gla[layer], n_s, s_s, s_s)
        x1_s, h2_s, ids_s, wts_s, rank_s, cnt = merge_and_route(
            ys, [mods_s[1], mods_s[0], mods_s[2], mods_s[4], mods_s[3]], olat_s, ogla_s, mw, cnt_p, alpha)

        h2 = jnp.concatenate([h2_p, h2_s], axis=0)
        ids = jnp.concatenate([ids_p, ids_s], axis=1)
        rank = jnp.concatenate([rank_p, rank_s], axis=1)
        tok_sorted, pos, pairs = dispatch(ids, rank, cnt)
        x_sorted = h2.at[tok_sorted].get(mode="promise_in_bounds")
        y_sorted = routed_expert_blocks(x_sorted, pairs, w["w_e_gate"], w["w_e_up"], w["w_e_down"])
        yg_p = y_sorted.at[pos[:, :t_p]].get(mode="promise_in_bounds")
        yg_s = y_sorted.at[pos[:, t_p:]].get(mode="promise_in_bounds")

        yp = shared_combine_norm(x1_p, mods_p[5], h2_p, yg_p, jnp.transpose(wts_p), fw, alpha)
        ys = shared_combine_norm(x1_s, mods_s[5], h2_s, yg_s, jnp.transpose(wts_s), fw, alpha)

        for lst, val in zip(outs, [ckv_p.reshape(n_p, s_p, KV_LORA), kr_p.reshape(n_p, s_p, QK_ROPE), sfin_p,
                                   ckv_s.reshape(n_s, s_s, KV_LORA), kr_s.reshape(n_s, s_s, QK_ROPE), sfin_s]):
            lst.append(val.astype(state_gla.dtype) if val.ndim == 4 else val)
    return (yp, ys) + tuple(jnp.stack(o) for o in outs)
```
